```python
import jax, jax.numpy as jnp
from jax import lax
import numpy as np

D_MODEL = 1024
BATCH = 2
SEQ = 16384
DEPTH = 4
DEC_BATCH = 8
DEC_SEQ = 64
PAST_LEN = 2048

CHUNK = 64
D_FF = 2816
A_HEADS = 4
A_HEAD = 128
A_WIDTH = A_HEADS * A_HEAD
GMLP_CHUNK = 128
B_GROUPS = 4
B_GROUP = 128
B_WIDTH = B_GROUPS * B_GROUP
POOL_WINDOWS = (2, 4, 8, 16)
POOL_BUF = 15
C_HEADS = 4
C_HEAD_K = 128
C_HEAD_V = 128
C_WIDTH = C_HEADS * C_HEAD_V
HGRN_BLOCK = CHUNK

N_BRANCH = 3
MIX_WIDTH = A_WIDTH + B_WIDTH + C_WIDTH
IN_COLS = 2 * A_WIDTH + B_WIDTH + 4 * C_WIDTH + N_BRANCH * D_MODEL
SPLIT_POINTS = (A_WIDTH, 2 * A_WIDTH, 2 * A_WIDTH + B_WIDTH,
                2 * A_WIDTH + B_WIDTH + C_WIDTH, 2 * A_WIDTH + B_WIDTH + 2 * C_WIDTH,
                2 * A_WIDTH + B_WIDTH + 3 * C_WIDTH, 2 * A_WIDTH + B_WIDTH + 4 * C_WIDTH)
EPS = 1e-6
F_FLOOR = 1e-30

kernel_name = 'hybrid_gmlp_pool_hgrn2_streaming_step'


def rms_norm(x, g):
    xf = x.astype(jnp.float32)
    y = xf * lax.rsqrt(jnp.mean(xf * xf, axis=-1, keepdims=True) + EPS)
    return (y * g.astype(jnp.float32)).astype(x.dtype)


def swiglu(x, w_gu, w_down):
    g, u = jnp.split(x @ w_gu, 2, axis=-1)
    return (jax.nn.silu(g) * u) @ w_down


def chunk_gmlp(u, v, w_s, b_s):
    bsz, t, _ = v.shape
    L = min(t, GMLP_CHUNK)
    n = t // L
    vb = v.reshape(bsz, n, L, A_HEADS, A_HEAD)
    w = jnp.tril(w_s[:, :L, :L])
    mixed = jnp.einsum('hts,bnshc->bnthc', w, vb) + b_s[:, :L].T[:, :, None]
    return u * mixed.reshape(bsz, t, A_WIDTH)


def multi_scale_pool(xb, prev, n_prev_valid, w_pool, scale):
    bsz, t, _ = xb.shape
    xp = jnp.concatenate([prev.astype(xb.dtype), xb], axis=1)
    cs = jnp.pad(jnp.cumsum(xp.astype(jnp.float32), axis=1), ((0, 0), (1, 0), (0, 0)))
    avail = jnp.arange(t) + 1 + n_prev_valid
    outs = []
    for g, w in enumerate(POOL_WINDOWS):
        sl = slice(g * B_GROUP, (g + 1) * B_GROUP)
        s = cs[:, POOL_BUF + 1:POOL_BUF + 1 + t, sl] - cs[:, POOL_BUF + 1 - w:POOL_BUF + 1 - w + t, sl]
        cnt = jnp.minimum(avail, w).astype(jnp.float32)
        outs.append(s / cnt[None, :, None] - xb[:, :, sl].astype(jnp.float32))
    pooled = jnp.stack(outs, axis=2)
    y = jnp.einsum('btgc,gcd->btgd', pooled, w_pool.astype(jnp.float32)).reshape(bsz, t, B_WIDTH)
    y = y * scale.astype(jnp.float32)
    return y.astype(xb.dtype), xp[:, -POOL_BUF:]


def hgrn_block(S0, q, k, v, logf):
    L = q.shape[1]
    b = jnp.cumsum(logf, axis=1)
    o_inter = jnp.einsum('blhk,bhkv->blhv', q * jnp.exp(b), S0)
    causal = jnp.tril(jnp.ones((L, L), dtype=bool))[None, :, :, None, None]
    diff = b[:, :, None] - b[:, None, :]
    decay = jnp.where(causal, jnp.exp(jnp.where(causal, diff, 0.0)), 0.0)
    scores = jnp.einsum('bthk,bshk,btshk->bhts', q, k, decay)
    o = o_inter + jnp.einsum('bhts,bshv->bthv', scores, v)
    b_last = b[:, -1]
    S = jnp.exp(b_last)[..., None] * S0 + jnp.einsum('bshk,bshv->bhkv', k * jnp.exp(b_last[:, None] - b), v)
    return S, o


def hgrn2(xq, xf, xi, xg, lb, S0, out_gain):
    bsz, t, _ = xq.shape
    f32 = jnp.float32
    q = jax.nn.silu(xq.astype(f32)) * (C_HEAD_K ** -0.5)
    z = xf.astype(f32)
    lb = lb.astype(f32)
    f = lb + (1.0 - lb) * jax.nn.sigmoid(z)
    logf = jnp.log(jnp.maximum(f, F_FLOOR))
    k = (1.0 - lb) * jax.nn.sigmoid(-z)
    v = xi.astype(f32)
    L = min(t, HGRN_BLOCK)
    n = t // L

    def to_blocks(a, d):
        return a.reshape(bsz, n, L, C_HEADS, d).transpose(1, 0, 2, 3, 4)

    S, o = lax.scan(lambda s, xs: hgrn_block(s, *xs), S0.astype(f32),
                    (to_blocks(q, C_HEAD_K), to_blocks(k, C_HEAD_K), to_blocks(v, C_HEAD_V), to_blocks(logf, C_HEAD_K)))
    o = o.transpose(1, 0, 2, 3, 4).reshape(bsz, t, C_HEADS, C_HEAD_V)
    o = rms_norm(o, out_gain) * jax.nn.silu(xg.astype(f32).reshape(bsz, t, C_HEADS, C_HEAD_V))
    return o.reshape(bsz, t, C_WIDTH).astype(xq.dtype), S


def trunk_layer(x, pool_prev, pool_valid, s_prev, lb, gains, w1_gu, w1_dn, w_in, v_gain, w_s, b_s,
                w_pool, pool_scale, c_gain, w_branch, w_out, w2_gu, w2_dn):
    bsz, t, _ = x.shape
    x = x + 0.5 * rms_norm(swiglu(rms_norm(x, gains[0]), w1_gu, w1_dn), gains[1])
    h = rms_norm(x, gains[2])
    xu, xv, xb, xq, xf, xi, xg, gates = jnp.split(h @ w_in, SPLIT_POINTS, axis=-1)
    u = jax.nn.gelu(xu)
    v = rms_norm(jax.nn.gelu(xv), v_gain)
    o_a = chunk_gmlp(u, v, w_s, b_s)
    o_b, pool_new = multi_scale_pool(xb, pool_prev, pool_valid, w_pool, pool_scale)
    o_c, s_new = hgrn2(xq, xf, xi, xg, lb, s_prev, c_gain)
    y_a = o_a @ w_branch[:A_WIDTH]
    y_b = o_b @ w_branch[A_WIDTH:A_WIDTH + B_WIDTH]
    y_c = o_c @ w_branch[A_WIDTH + B_WIDTH:]
    g = jax.nn.sigmoid(gates.astype(jnp.float32)).astype(x.dtype).reshape(bsz, t, N_BRANCH, D_MODEL)
    merged = g[:, :, 0] * y_a + g[:, :, 1] * y_b + g[:, :, 2] * y_c
    x = x + rms_norm(merged @ w_out, gains[3])
    x = x + 0.5 * rms_norm(swiglu(rms_norm(x, gains[4]), w2_gu, w2_dn), gains[5])
    return x, pool_new, s_new, v


def setup_inputs(seed: int = 0) -> dict:
    key = jax.random.key(seed)
    ks = jax.random.split(key, 20)

    def nrm(k, shape, scale):
        return jax.random.normal(k, shape, jnp.float32) * scale

    return {
        'x_prompt': nrm(ks[0], (BATCH, SEQ, D_MODEL), 1.0),
        'x_sample': nrm(ks[1], (DEC_BATCH, DEC_SEQ, D_MODEL), 1.0),
        'state_pool': nrm(ks[2], (DEPTH, DEC_BATCH, POOL_BUF, B_WIDTH), 1.0),
        'state_hgrn': nrm(ks[3], (DEPTH, DEC_BATCH, C_HEADS, C_HEAD_K, C_HEAD_V), 0.5),
        'norm_gains': 1.0 + nrm(ks[4], (DEPTH, 6, D_MODEL), 0.02),
        'w_ffn1_gu': nrm(ks[5], (DEPTH, D_MODEL, 2 * D_FF), D_MODEL ** -0.5),
        'w_ffn1_down': nrm(ks[6], (DEPTH, D_FF, D_MODEL), D_FF ** -0.5),
        'w_in': nrm(ks[7], (DEPTH, D_MODEL, IN_COLS), D_MODEL ** -0.5),
        'gmlp_v_gain': 1.0 + nrm(ks[8], (DEPTH, A_WIDTH), 0.02),
        'gmlp_w_s': nrm(ks[9], (DEPTH, A_HEADS, GMLP_CHUNK, GMLP_CHUNK), GMLP_CHUNK ** -0.5),
        'gmlp_b_s': 1.0 + nrm(ks[10], (DEPTH, A_HEADS, GMLP_CHUNK), 0.02),
        'pool_w': nrm(ks[11], (DEPTH, B_GROUPS, B_GROUP, B_GROUP), B_GROUP ** -0.5),
        'pool_scale': 1.0 + nrm(ks[12], (DEPTH, B_WIDTH), 0.1),
        'hgrn_lower_bounds': nrm(ks[13], (DEPTH, C_WIDTH), 0.5),
        'hgrn_out_gain': 1.0 + nrm(ks[14], (DEPTH, C_HEAD_V), 0.02),
        'w_branch': nrm(ks[15], (DEPTH, MIX_WIDTH, D_MODEL), A_WIDTH ** -0.5),
        'w_out': nrm(ks[16], (DEPTH, D_MODEL, D_MODEL), D_MODEL ** -0.5),
        'w_ffn2_gu': nrm(ks[17], (DEPTH, D_MODEL, 2 * D_FF), D_MODEL ** -0.5),
        'w_ffn2_down': nrm(ks[18], (DEPTH, D_FF, D_MODEL), D_FF ** -0.5),
    }


def reference(x_prompt, x_sample, state_pool, state_hgrn, norm_gains, w_ffn1_gu, w_ffn1_down, w_in,
              gmlp_v_gain, gmlp_w_s, gmlp_b_s, pool_w, pool_scale, hgrn_lower_bounds, hgrn_out_gain,
              w_branch, w_out, w_ffn2_gu, w_ffn2_down):
    lb_soft = jax.nn.softmax(hgrn_lower_bounds.astype(jnp.float32), axis=0)
    lb_all = jnp.cumsum(lb_soft, axis=0) - lb_soft[0]
    bp = x_prompt.shape[0]
    zero_pool = jnp.zeros((bp, POOL_BUF, B_WIDTH), x_prompt.dtype)
    zero_state = jnp.zeros((bp, C_HEADS, C_HEAD_K, C_HEAD_V), jnp.float32)
    xp, xs = x_prompt, x_sample
    pool_p, pool_s, hg_p, hg_s, v_s = [], [], [], [], []
    for l in range(DEPTH):
        weights = (lb_all[l], norm_gains[l], w_ffn1_gu[l], w_ffn1_down[l], w_in[l], gmlp_v_gain[l],
                   gmlp_w_s[l], gmlp_b_s[l], pool_w[l], pool_scale[l], hgrn_out_gain[l], w_branch[l],
                   w_out[l], w_ffn2_gu[l], w_ffn2_down[l])
        xp, pp, sp, _ = trunk_layer(xp, zero_pool, 0, zero_state, *weights)
        xs, ps, ss, vs = trunk_layer(xs, state_pool[l], POOL_BUF, state_hgrn[l], *weights)
        pool_p.append(pp)
        pool_s.append(ps)
        hg_p.append(sp.astype(state_hgrn.dtype))
        hg_s.append(ss.astype(state_hgrn.dtype))
        v_s.append(vs)
    pool_prompt = jnp.stack(pool_p, axis=0)
    pool_sample = jnp.stack(pool_s, axis=0)
    hgrn_prompt = jnp.stack(hg_p, axis=0)
    hgrn_sample = jnp.stack(hg_s, axis=0)
    gmlp_v_sample = jnp.stack(v_s, axis=0)
    return (xp, xs, pool_prompt, pool_sample, hgrn_prompt, hgrn_sample, gmlp_v_sample)
```

```python
import functools

import jax
import jax.numpy as jnp
from jax import lax
from jax.experimental import pallas as pl
from jax.experimental.pallas import tpu as pltpu

D_MODEL = 1024
DEPTH = 4
D_FF = 2816
A_HEADS = 4
A_HEAD = 128
A_WIDTH = A_HEADS * A_HEAD
GMLP_CHUNK = 128
B_GROUPS = 4
B_GROUP = 128
B_WIDTH = B_GROUPS * B_GROUP
POOL_WINDOWS = (2, 4, 8, 16)
POOL_BUF = 15
POOL_PAD = 16
C_HEADS = 4
C_HEAD_K = 128
C_HEAD_V = 128
C_WIDTH = C_HEADS * C_HEAD_V
N_BRANCH = 3
EPS = 1e-6
F_FLOOR = 1e-30

OFF_UV = 0
OFF_POOL = 2 * A_WIDTH
OFF_HGRN = OFF_POOL + B_WIDTH
OFF_GATES = OFF_HGRN + 4 * C_WIDTH
IN_COLS = OFF_GATES + N_BRANCH * D_MODEL

SUBLANES = 8
FFN_ROWS = 512
MIX_ROWS = 256
VMEM_LIMIT = 56 * 1024 * 1024

F32 = jnp.float32
BF16 = jnp.bfloat16


def _rms(x, g):
    ms = jnp.mean(x * x, axis=-1, keepdims=True)
    return x * lax.rsqrt(ms + EPS) * g


def _dot(a, b):
    return jnp.dot(a, b, preferred_element_type=F32)


def _dot_nt(a, b):
    return lax.dot_general(a, b, (((1,), (1,)), ((), ())), preferred_element_type=F32)


def _dot_tn(a, b):
    return lax.dot_general(a, b, (((0,), (0,)), ((), ())), preferred_element_type=F32)


def _ffn_kernel(x_ref, g_ref, wgu_ref, wdn_ref, o_ref, *, pre, post):
    x = x_ref[...]
    h = _rms(x, g_ref[pre:pre + 1, :]).astype(BF16)
    g = _dot(h, wgu_ref[:, :D_FF])
    u = _dot(h, wgu_ref[:, D_FF:])
    a = (jax.nn.silu(g) * u).astype(BF16)
    y = _dot(a, wdn_ref[...])
    o_ref[...] = x + 0.5 * _rms(y, g_ref[post:post + 1, :])


def _ffn(x2d, gains, wgu, wdn, layer, pre, post):
    rows = x2d.shape[0]
    tm = min(FFN_ROWS, rows)
    const = dict(pipeline_mode=pl.Buffered(1))
    return pl.pallas_call(
        functools.partial(_ffn_kernel, pre=pre, post=post),
        grid=(rows // tm,),
        in_specs=[
            pl.BlockSpec((tm, D_MODEL), lambda i: (i, 0)),
            pl.BlockSpec((None, 6, D_MODEL), lambda i: (layer, 0, 0), **const),
            pl.BlockSpec((None, D_MODEL, 2 * D_FF), lambda i: (layer, 0, 0), **const),
            pl.BlockSpec((None, D_FF, D_MODEL), lambda i: (layer, 0, 0), **const),
        ],
        out_specs=pl.BlockSpec((tm, D_MODEL), lambda i: (i, 0)),
        out_shape=jax.ShapeDtypeStruct(x2d.shape, F32),
        compiler_params=pltpu.CompilerParams(
            dimension_semantics=("arbitrary",), vmem_limit_bytes=VMEM_LIMIT),
        name=f"ffn_l{layer}_{pre}",
    )(x2d, gains, wgu, wdn)


def _cumsum_rows(x):
    n = x.shape[0]
    row = lax.broadcasted_iota(jnp.int32, x.shape, 0)
    s = 1
    while s < n:
        x = x + jnp.where(row >= s, pltpu.roll(x, s, axis=0), 0.0)
        s *= 2
    return x


def _block_row(x, blk, r):
    n, w = x.shape
    if blk >= SUBLANES:
        y = x.reshape(n // blk, blk, w)[:, r:r + 1, :]
        return jnp.broadcast_to(y, (n // blk, blk, w)).reshape(n, w)
    row = lax.broadcasted_iota(jnp.int32, x.shape, 0) & (SUBLANES - 1)
    if blk == 4:
        assert r == 1
        return jnp.where(row < 4, _block_row(x, SUBLANES, 1), _block_row(x, SUBLANES, 5))
    assert blk == 2 and r == 0
    return jnp.where((row & 1) == 1, pltpu.roll(x, 1, axis=0), x)


def _hgrn_chunk(q, k, logf, v, st_ref, seq):
    n = q.shape[0]
    row = lax.broadcasted_iota(jnp.int32, (n, C_WIDTH), 0)
    rr = lax.broadcasted_iota(jnp.int32, (n, n), 0)
    cc = lax.broadcasted_iota(jnp.int32, (n, n), 1)
    b = _cumsum_rows(logf)
    scores = [jnp.zeros((n, n), F32) for _ in range(C_HEADS)]
    h = n // 2
    while h >= 1:
        blk = 2 * h
        bmid = _block_row(b, blk, h - 1)
        second = (row & (blk - 1)) >= h
        ql = jnp.where(second, q * jnp.exp(jnp.minimum(b - bmid, 0.0)), 0.0).astype(BF16)
        kl = jnp.where(second, 0.0, k * jnp.exp(jnp.minimum(bmid - b, 0.0))).astype(BF16)
        same = None if blk == n else (rr & -blk) == (cc & -blk)
        for hd in range(C_HEADS):
            sl = slice(hd * C_HEAD_K, (hd + 1) * C_HEAD_K)
            s = _dot_nt(ql[:, sl], kl[:, sl])
            scores[hd] = scores[hd] + (s if same is None else jnp.where(same, s, 0.0))
        h //= 2
    qd = q.astype(BF16)
    kd = k.astype(BF16)
    vb = v.astype(BF16)
    b_last = b[n - 1:n, :]
    qb = (q * jnp.exp(b)).astype(BF16)
    kb = (k * jnp.exp(b_last - b)).astype(BF16)
    decay = jnp.exp(b_last)
    outs = []
    for hd in range(C_HEADS):
        sl = slice(hd * C_HEAD_K, (hd + 1) * C_HEAD_K)
        sc = scores[hd] + jnp.where(rr == cc, _dot_nt(qd[:, sl], kd[:, sl]), 0.0)
        st = st_ref[seq, hd]
        outs.append(_dot(sc.astype(BF16), vb[:, sl]) + _dot_nt(qb[:, sl], st.astype(BF16)))
        st_ref[seq, hd] = decay[:, sl] * st + _dot_tn(vb[:, sl], kb[:, sl])
    return jnp.concatenate(outs, axis=1)


def _mixer_kernel(x_ref, pool0_ref, s0_ref, g_ref, win_ref, vgain_ref, ws_ref, bs_ref, wpool_ref,
                  pscale_ref, lbraw_ref, cgain_ref, wbr_ref, wout_ref,
                  xo_ref, poolo_ref, so_ref, *rest,
                  layer, n_seq, seq_tile, chunk, n_valid, emit_v):
    if emit_v:
        vo_ref, rest = rest[0], rest[1:]
    u_s, v_s, q_s, k_s, lf_s, vi_s, oa_s, oc_s, st_s, ext_s = rest
    rows = n_seq * seq_tile
    t_idx = pl.program_id(1)

    @pl.when(t_idx == 0)
    def _init():
        for s in range(n_seq):
            for hd in range(C_HEADS):
                st_s[s, hd] = s0_ref[s, hd].T
            ext_s[s, 0:1, :] = jnp.zeros((1, B_WIDTH), F32)
            ext_s[s, POOL_PAD - POOL_BUF:POOL_PAD, :] = pool0_ref[s]

    x = x_ref[...].reshape(rows, D_MODEL)
    h = _rms(x, g_ref[2:3, :]).astype(BF16)

    pa = _dot(h, win_ref[:, OFF_UV:OFF_UV + 2 * A_WIDTH])
    u_s[...] = jax.nn.gelu(pa[:, :A_WIDTH])
    v = _rms(jax.nn.gelu(pa[:, A_WIDTH:]), vgain_ref[...])
    v_s[...] = v
    if emit_v:
        vo_ref[...] = v.reshape(n_seq, seq_tile, A_WIDTH)

    xb = _dot(h, win_ref[:, OFF_POOL:OFF_POOL + B_WIDTH])
    t_loc = lax.broadcasted_iota(jnp.int32, (seq_tile, 1), 0)
    avail = t_idx * seq_tile + t_loc + (1 + n_valid)
    ob_rows = []
    for s in range(n_seq):
        ext_s[s, POOL_PAD:POOL_PAD + seq_tile, :] = xb[s * seq_tile:(s + 1) * seq_tile]
        parts = []
        for g, w in enumerate(POOL_WINDOWS):
            sl = slice(g * B_GROUP, (g + 1) * B_GROUP)
            x0 = ext_s[s, POOL_PAD:POOL_PAD + seq_tile, sl]
            acc = x0
            for j in range(1, w):
                acc = acc + ext_s[s, POOL_PAD - j:POOL_PAD - j + seq_tile, sl]
            inv = 1.0 / jnp.minimum(avail, w).astype(F32)
            pooled = acc * inv - x0
            parts.append(_dot(pooled.astype(BF16), wpool_ref[g].astype(BF16)))
        ob_rows.append(jnp.concatenate(parts, axis=1))
        poolo_ref[s] = ext_s[s, seq_tile + POOL_PAD - POOL_BUF:seq_tile + POOL_PAD, :]
        ext_s[s, 0:POOL_PAD, :] = ext_s[s, seq_tile:seq_tile + POOL_PAD, :]
    o_b = jnp.concatenate(ob_rows, axis=0) * pscale_ref[...]

    lbr = lbraw_ref[...]
    e = jnp.exp(lbr - jnp.max(lbr, axis=0, keepdims=True))
    soft = e / jnp.sum(e, axis=0, keepdims=True)
    lb = jnp.zeros((1, C_WIDTH), F32)
    for j in range(1, layer + 1):
        lb = lb + soft[j:j + 1, :]
    pq = _dot(h, win_ref[:, OFF_HGRN:OFF_HGRN + 4 * C_WIDTH])
    q_s[...] = jax.nn.silu(pq[:, :C_WIDTH]) * (C_HEAD_K ** -0.5)
    z = pq[:, C_WIDTH:2 * C_WIDTH]
    f = lb + (1.0 - lb) * jax.nn.sigmoid(z)
    lf_s[...] = jnp.log(jnp.maximum(f, F_FLOOR))
    k_s[...] = (1.0 - lb) * jax.nn.sigmoid(-z)
    vi_s[...] = pq[:, 2 * C_WIDTH:3 * C_WIDTH]
    xg = pq[:, 3 * C_WIDTH:]

    tri = (lax.broadcasted_iota(jnp.int32, (chunk, chunk), 0)
           >= lax.broadcasted_iota(jnp.int32, (chunk, chunk), 1))
    w_mix = [jnp.where(tri, ws_ref[hd, 0:chunk, 0:chunk], 0.0).astype(BF16) for hd in range(A_HEADS)]
    chunks_per_seq = seq_tile // chunk

    def chunk_body(c, carry):
        r0 = pl.multiple_of(c * chunk, chunk)
        rs = pl.ds(r0, chunk)
        vb = v_s[rs, :].astype(BF16)
        for hd in range(A_HEADS):
            sl = slice(hd * A_HEAD, (hd + 1) * A_HEAD)
            mixed = _dot(w_mix[hd], vb[:, sl]) + bs_ref[0:chunk, hd:hd + 1]
            oa_s[rs, sl] = u_s[rs, sl] * mixed
        oc_s[rs, :] = _hgrn_chunk(q_s[rs, :], k_s[rs, :], lf_s[rs, :], vi_s[rs, :], st_s,
                                  c // chunks_per_seq)
        return carry

    lax.fori_loop(0, rows // chunk, chunk_body, 0)

    oc = oc_s[...]
    cg = cgain_ref[...]
    oc = jnp.concatenate(
        [_rms(oc[:, hd * C_HEAD_V:(hd + 1) * C_HEAD_V], cg) for hd in range(C_HEADS)], axis=1)
    o_c = oc * jax.nn.silu(xg)

    def gate(i):
        off = OFF_GATES + i * D_MODEL
        return jax.nn.sigmoid(_dot(h, win_ref[:, off:off + D_MODEL]))

    merged = gate(0) * _dot(oa_s[...].astype(BF16), wbr_ref[0:A_WIDTH, :])
    merged = merged + gate(1) * _dot(o_b.astype(BF16), wbr_ref[A_WIDTH:A_WIDTH + B_WIDTH, :])
    merged = merged + gate(2) * _dot(o_c.astype(BF16), wbr_ref[A_WIDTH + B_WIDTH:, :])
    y = _dot(merged.astype(BF16), wout_ref[...])
    xo_ref[...] = (x + _rms(y, g_ref[3:4, :])).reshape(n_seq, seq_tile, D_MODEL)

    @pl.when(t_idx == pl.num_programs(1) - 1)
    def _final():
        for s in range(n_seq):
            for hd in range(C_HEADS):
                so_ref[s, hd] = st_s[s, hd].T


def _mixer(x, pool0, s0, gains, w_in, vgain, w_s, b_sT, w_pool, pscale, lb_raw, cgain, w_br, w_out,
           layer, n_seq, seq_tile, n_valid, emit_v):
    bsz, t, _ = x.shape
    chunk = min(seq_tile, GMLP_CHUNK)
    rows = n_seq * seq_tile
    grid = (bsz // n_seq, t // seq_tile)
    const = dict(pipeline_mode=pl.Buffered(1))

    def lsel(*tail):
        return lambda b, i: (layer,) + tail

    in_specs = [
        pl.BlockSpec((n_seq, seq_tile, D_MODEL), lambda b, i: (b, i, 0)),
        pl.BlockSpec((n_seq, POOL_BUF, B_WIDTH), lambda b, i: (b, 0, 0)),
        pl.BlockSpec((n_seq, C_HEADS, C_HEAD_K, C_HEAD_V), lambda b, i: (b, 0, 0, 0)),
        pl.BlockSpec((None, 6, D_MODEL), lsel(0, 0), **const),
        pl.BlockSpec((None, D_MODEL, IN_COLS), lsel(0, 0), **const),
        pl.BlockSpec((None, 1, A_WIDTH), lsel(0, 0), **const),
        pl.BlockSpec((None, A_HEADS, GMLP_CHUNK, GMLP_CHUNK), lsel(0, 0, 0), **const),
        pl.BlockSpec((None, GMLP_CHUNK, A_HEADS), lsel(0, 0), **const),
        pl.BlockSpec((None, B_GROUPS, B_GROUP, B_GROUP), lsel(0, 0, 0), **const),
        pl.BlockSpec((None, 1, B_WIDTH), lsel(0, 0), **const),
        pl.BlockSpec((DEPTH, C_WIDTH), lambda b, i: (0, 0), **const),
        pl.BlockSpec((None, 1, C_HEAD_V), lsel(0, 0), **const),
        pl.BlockSpec((None, N_BRANCH * A_WIDTH, D_MODEL), lsel(0, 0), **const),
        pl.BlockSpec((None, D_MODEL, D_MODEL), lsel(0, 0), **const),
    ]
    out_specs = [
        pl.BlockSpec((n_seq, seq_tile, D_MODEL), lambda b, i: (b, i, 0)),
        pl.BlockSpec((n_seq, POOL_BUF, B_WIDTH), lambda b, i: (b, 0, 0)),
        pl.BlockSpec((n_seq, C_HEADS, C_HEAD_K, C_HEAD_V), lambda b, i: (b, 0, 0, 0)),
    ]
    out_shape = [
        jax.ShapeDtypeStruct(x.shape, F32),
        jax.ShapeDtypeStruct((bsz, POOL_BUF, B_WIDTH), F32),
        jax.ShapeDtypeStruct((bsz, C_HEADS, C_HEAD_K, C_HEAD_V), F32),
    ]
    if emit_v:
        out_specs.append(pl.BlockSpec((n_seq, seq_tile, A_WIDTH), lambda b, i: (b, i, 0)))
        out_shape.append(jax.ShapeDtypeStruct((bsz, t, A_WIDTH), F32))
    act = pltpu.VMEM((rows, C_WIDTH), F32)
    scratch = [act] * 8 + [
        pltpu.VMEM((n_seq, C_HEADS, C_HEAD_V, C_HEAD_K), F32),
        pltpu.VMEM((n_seq, POOL_PAD + seq_tile, B_WIDTH), F32),
    ]
    return pl.pallas_call(
        functools.partial(_mixer_kernel, layer=layer, n_seq=n_seq, seq_tile=seq_tile, chunk=chunk,
                          n_valid=n_valid, emit_v=emit_v),
        grid=grid,
        in_specs=in_specs,
        out_specs=out_specs,
        out_shape=out_shape,
        scratch_shapes=scratch,
        compiler_params=pltpu.CompilerParams(
            dimension_semantics=("arbitrary", "arbitrary"), vmem_limit_bytes=VMEM_LIMIT),
        name=f"mixer_l{layer}_{'dec' if emit_v else 'pre'}",
    )(x, pool0, s0, gains, w_in, vgain, w_s, b_sT, w_pool, pscale, lb_raw, cgain, w_br, w_out)


def kernel(x_prompt, x_sample, state_pool, state_hgrn, norm_gains, w_ffn1_gu, w_ffn1_down, w_in,
           gmlp_v_gain, gmlp_w_s, gmlp_b_s, pool_w, pool_scale, hgrn_lower_bounds, hgrn_out_gain,
           w_branch, w_out, w_ffn2_gu, w_ffn2_down):
    bp, tp, _ = x_prompt.shape
    bs, ts, _ = x_sample.shape
    w1gu, w1dn = w_ffn1_gu.astype(BF16), w_ffn1_down.astype(BF16)
    w2gu, w2dn = w_ffn2_gu.astype(BF16), w_ffn2_down.astype(BF16)
    win, wbr, wout = w_in.astype(BF16), w_branch.astype(BF16), w_out.astype(BF16)
    vgain = gmlp_v_gain.reshape(DEPTH, 1, A_WIDTH)
    pscale = pool_scale.reshape(DEPTH, 1, B_WIDTH)
    cgain = hgrn_out_gain.reshape(DEPTH, 1, C_HEAD_V)
    b_sT = jnp.swapaxes(gmlp_b_s, 1, 2)
    zero_pool = jnp.zeros((bp, POOL_BUF, B_WIDTH), F32)
    zero_state = jnp.zeros((bp, C_HEADS, C_HEAD_K, C_HEAD_V), F32)

    xp, xs = x_prompt, x_sample
    pool_p, pool_s, hg_p, hg_s, v_s = [], [], [], [], []
    for l in range(DEPTH):
        shared = (norm_gains, win, vgain, gmlp_w_s, b_sT, pool_w, pscale, hgrn_lower_bounds, cgain,
                  wbr, wout)
        xp = _ffn(xp.reshape(bp * tp, D_MODEL), norm_gains, w1gu, w1dn, l, 0, 1).reshape(bp, tp, D_MODEL)
        xs = _ffn(xs.reshape(bs * ts, D_MODEL), norm_gains, w1gu, w1dn, l, 0, 1).reshape(bs, ts, D_MODEL)
        xp, pp, sp = _mixer(xp, zero_pool, zero_state, *shared, layer=l, n_seq=1,
                            seq_tile=min(MIX_ROWS, tp), n_valid=0, emit_v=False)
        xs, ps, ss, vs = _mixer(xs, state_pool[l], state_hgrn[l], *shared, layer=l, n_seq=bs,
                                seq_tile=ts, n_valid=POOL_BUF, emit_v=True)
        xp = _ffn(xp.reshape(bp * tp, D_MODEL), norm_gains, w2gu, w2dn, l, 4, 5).reshape(bp, tp, D_MODEL)
        xs = _ffn(xs.reshape(bs * ts, D_MODEL), norm_gains, w2gu, w2dn, l, 4, 5).reshape(bs, ts, D_MODEL)
        pool_p.append(pp)
        pool_s.append(ps)
        hg_p.append(sp)
        hg_s.append(ss)
        v_s.append(vs)
    return (xp, xs, jnp.stack(pool_p), jnp.stack(pool_s), jnp.stack(hg_p), jnp.stack(hg_s),
            jnp.stack(v_s))
```

```python
import functools

import jax
import jax.numpy as jnp
from jax import lax
from jax.experimental import pallas as pl
from jax.experimental.pallas import tpu as pltpu

D_MODEL = 1024
DEPTH = 4
D_FF = 2816
A_HEADS = 4
A_HEAD = 128
A_WIDTH = A_HEADS * A_HEAD
GMLP_CHUNK = 128
B_GROUPS = 4
B_GROUP = 128
B_WIDTH = B_GROUPS * B_GROUP
POOL_WINDOWS = (2, 4, 8, 16)
POOL_BUF = 15
POOL_PAD = 16
C_HEADS = 4
C_HEAD_K = 128
C_HEAD_V = 128
C_WIDTH = C_HEADS * C_HEAD_V
N_BRANCH = 3
EPS = 1e-6
F_FLOOR = 1e-30

OFF_UV = 0
OFF_POOL = 2 * A_WIDTH
OFF_HGRN = OFF_POOL + B_WIDTH
OFF_GATES = OFF_HGRN + 4 * C_WIDTH
IN_COLS = OFF_GATES + N_BRANCH * D_MODEL

SUBLANES = 8
FFN_ROWS = 512
MIX_ROWS = 256
VMEM_LIMIT = 56 * 1024 * 1024

F32 = jnp.float32
BF16 = jnp.bfloat16


def _rms(x, g):
    ms = jnp.mean(x * x, axis=-1, keepdims=True)
    return x * lax.rsqrt(ms + EPS) * g


def _dot(a, b):
    return jnp.dot(a, b, preferred_element_type=F32)


def _dot_nt(a, b):
    return lax.dot_general(a, b, (((1,), (1,)), ((), ())), preferred_element_type=F32)


def _dot_tn(a, b):
    return lax.dot_general(a, b, (((0,), (0,)), ((), ())), preferred_element_type=F32)


def _ffn_kernel(x_ref, g_ref, wgu_ref, wdn_ref, o_ref, *, pre, post):
    x = x_ref[...]
    h = _rms(x, g_ref[pre:pre + 1, :]).astype(BF16)
    g = _dot(h, wgu_ref[:, :D_FF])
    u = _dot(h, wgu_ref[:, D_FF:])
    a = (jax.nn.silu(g) * u).astype(BF16)
    y = _dot(a, wdn_ref[...])
    o_ref[...] = x + 0.5 * _rms(y, g_ref[post:post + 1, :])


def _ffn(x2d, gains, wgu, wdn, layer, pre, post):
    rows = x2d.shape[0]
    tm = min(FFN_ROWS, rows)
    const = dict(pipeline_mode=pl.Buffered(1))
    return pl.pallas_call(
        functools.partial(_ffn_kernel, pre=pre, post=post),
        grid=(rows // tm,),
        in_specs=[
            pl.BlockSpec((tm, D_MODEL), lambda i: (i, 0)),
            pl.BlockSpec((None, 6, D_MODEL), lambda i: (layer, 0, 0), **const),
            pl.BlockSpec((None, D_MODEL, 2 * D_FF), lambda i: (layer, 0, 0), **const),
            pl.BlockSpec((None, D_FF, D_MODEL), lambda i: (layer, 0, 0), **const),
        ],
        out_specs=pl.BlockSpec((tm, D_MODEL), lambda i: (i, 0)),
        out_shape=jax.ShapeDtypeStruct(x2d.shape, F32),
        compiler_params=pltpu.CompilerParams(
            dimension_semantics=("arbitrary",), vmem_limit_bytes=VMEM_LIMIT),
        name=f"ffn_l{layer}_{pre}",
    )(x2d, gains, wgu, wdn)


def _cumsum_rows(x):
    n = x.shape[0]
    row = lax.broadcasted_iota(jnp.int32, x.shape, 0)
    s = 1
    while s < n:
        x = x + jnp.where(row >= s, pltpu.roll(x, s, axis=0), 0.0)
        s *= 2
    return x


def _block_row(x, blk, r):
    n, w = x.shape
    if blk >= SUBLANES:
        y = x.reshape(n // blk, blk, w)[:, r:r + 1, :]
        return jnp.broadcast_to(y, (n // blk, blk, w)).reshape(n, w)
    row = lax.broadcasted_iota(jnp.int32, x.shape, 0) & (SUBLANES - 1)
    if blk == 4:
        assert r == 1
        return jnp.where(row < 4, _block_row(x, SUBLANES, 1), _block_row(x, SUBLANES, 5))
    assert blk == 2 and r == 0
    return jnp.where((row & 1) == 1, pltpu.roll(x, 1, axis=0), x)


def _hgrn_chunk(q, k, logf, v, st_ref, seq):
    n = q.shape[0]
    row = lax.broadcasted_iota(jnp.int32, (n, C_WIDTH), 0)
    rr = lax.broadcasted_iota(jnp.int32, (n, n), 0)
    cc = lax.broadcasted_iota(jnp.int32, (n, n), 1)
    xr = rr ^ cc
    lower = rr > cc
    heads = [slice(hd * C_HEAD_K, (hd + 1) * C_HEAD_K) for hd in range(C_HEADS)]
    b = _cumsum_rows(logf)
    p0 = q * k
    p1 = q * jnp.exp(logf) * pltpu.roll(k, 1, axis=0)
    scores = []
    for sl in heads:
        d0 = jnp.sum(p0[:, sl], axis=-1, keepdims=True)
        d1 = jnp.sum(p1[:, sl], axis=-1, keepdims=True)
        scores.append(jnp.where(rr == cc, d0, jnp.where((xr == 1) & lower, d1, 0.0)))
    h = n // 2
    while h >= 2:
        bmid = _block_row(b, 2 * h, h - 1)
        m = jnp.where((row & h) != 0, q, k) * jnp.exp(-jnp.abs(b - bmid))
        m = m.astype(BF16)
        pair = ((xr & -h) == h) & lower
        for hd, sl in enumerate(heads):
            scores[hd] = jnp.where(pair, _dot_nt(m[:, sl], m[:, sl]), scores[hd])
        h //= 2
    vb = v.astype(BF16)
    b_last = b[n - 1:n, :]
    qb = (q * jnp.exp(b)).astype(BF16)
    kb = (k * jnp.exp(b_last - b)).astype(BF16)
    decay = jnp.exp(b_last)
    outs = []
    for hd, sl in enumerate(heads):
        st = st_ref[seq, hd]
        outs.append(_dot(scores[hd].astype(BF16), vb[:, sl]) + _dot_nt(qb[:, sl], st.astype(BF16)))
        st_ref[seq, hd] = decay[:, sl] * st + _dot_tn(vb[:, sl], kb[:, sl])
    return jnp.concatenate(outs, axis=1)


def _mixer_kernel(x_ref, pool0_ref, s0_ref, g_ref, win_ref, vgain_ref, ws_ref, bs_ref, wpool_ref,
                  pscale_ref, lbraw_ref, cgain_ref, wbr_ref, wout_ref,
                  xo_ref, poolo_ref, so_ref, *rest,
                  layer, n_seq, seq_tile, chunk, n_valid, emit_v):
    if emit_v:
        vo_ref, rest = rest[0], rest[1:]
    u_s, v_s, q_s, k_s, lf_s, vi_s, oa_s, oc_s, st_s, ext_s = rest
    rows = n_seq * seq_tile
    t_idx = pl.program_id(1)

    @pl.when(t_idx == 0)
    def _init():
        for s in range(n_seq):
            for hd in range(C_HEADS):
                st_s[s, hd] = s0_ref[s, hd].T
            ext_s[s, 0:1, :] = jnp.zeros((1, B_WIDTH), F32)
            ext_s[s, POOL_PAD - POOL_BUF:POOL_PAD, :] = pool0_ref[s]

    x = x_ref[...].reshape(rows, D_MODEL)
    h = _rms(x, g_ref[2:3, :]).astype(BF16)

    pa = _dot(h, win_ref[:, OFF_UV:OFF_UV + 2 * A_WIDTH])
    u_s[...] = jax.nn.gelu(pa[:, :A_WIDTH])
    v = _rms(jax.nn.gelu(pa[:, A_WIDTH:]), vgain_ref[...])
    v_s[...] = v
    if emit_v:
        vo_ref[...] = v.reshape(n_seq, seq_tile, A_WIDTH)

    xb = _dot(h, win_ref[:, OFF_POOL:OFF_POOL + B_WIDTH])
    t_loc = lax.broadcasted_iota(jnp.int32, (seq_tile, 1), 0)
    avail = t_idx * seq_tile + t_loc + (1 + n_valid)
    ob_rows = []
    for s in range(n_seq):
        ext_s[s, POOL_PAD:POOL_PAD + seq_tile, :] = xb[s * seq_tile:(s + 1) * seq_tile]
        parts = []
        for g, w in enumerate(POOL_WINDOWS):
            sl = slice(g * B_GROUP, (g + 1) * B_GROUP)
            x0 = ext_s[s, POOL_PAD:POOL_PAD + seq_tile, sl]
            acc = x0
            for j in range(1, w):
                acc = acc + ext_s[s, POOL_PAD - j:POOL_PAD - j + seq_tile, sl]
            inv = 1.0 / jnp.minimum(avail, w).astype(F32)
            pooled = acc * inv - x0
            parts.append(_dot(pooled.astype(BF16), wpool_ref[g].astype(BF16)))
        ob_rows.append(jnp.concatenate(parts, axis=1))
        poolo_ref[s] = ext_s[s, seq_tile + POOL_PAD - POOL_BUF:seq_tile + POOL_PAD, :]
        ext_s[s, 0:POOL_PAD, :] = ext_s[s, seq_tile:seq_tile + POOL_PAD, :]
    o_b = jnp.concatenate(ob_rows, axis=0) * pscale_ref[...]

    lbr = lbraw_ref[...]
    e = jnp.exp(lbr - jnp.max(lbr, axis=0, keepdims=True))
    soft = e / jnp.sum(e, axis=0, keepdims=True)
    lb = jnp.zeros((1, C_WIDTH), F32)
    for j in range(1, layer + 1):
        lb = lb + soft[j:j + 1, :]
    pq = _dot(h, win_ref[:, OFF_HGRN:OFF_HGRN + 4 * C_WIDTH])
    q_s[...] = jax.nn.silu(pq[:, :C_WIDTH]) * (C_HEAD_K ** -0.5)
    z = pq[:, C_WIDTH:2 * C_WIDTH]
    f = lb + (1.0 - lb) * jax.nn.sigmoid(z)
    lf_s[...] = jnp.log(jnp.maximum(f, F_FLOOR))
    k_s[...] = (1.0 - lb) * jax.nn.sigmoid(-z)
    vi_s[...] = pq[:, 2 * C_WIDTH:3 * C_WIDTH]
    xg = pq[:, 3 * C_WIDTH:]

    tri = (lax.broadcasted_iota(jnp.int32, (chunk, chunk), 0)
           >= lax.broadcasted_iota(jnp.int32, (chunk, chunk), 1))
    w_mix = [jnp.where(tri, ws_ref[hd, 0:chunk, 0:chunk], 0.0).astype(BF16) for hd in range(A_HEADS)]
    chunks_per_seq = seq_tile // chunk

    for c in range(rows // chunk):
        rs = slice(c * chunk, (c + 1) * chunk)
        vb = v_s[rs, :].astype(BF16)
        for hd in range(A_HEADS):
            sl = slice(hd * A_HEAD, (hd + 1) * A_HEAD)
            mixed = _dot(w_mix[hd], vb[:, sl]) + bs_ref[0:chunk, hd:hd + 1]
            oa_s[rs, sl] = u_s[rs, sl] * mixed
        oc_s[rs, :] = _hgrn_chunk(q_s[rs, :], k_s[rs, :], lf_s[rs, :], vi_s[rs, :], st_s,
                                  c // chunks_per_seq)

    oc = oc_s[...]
    cg = cgain_ref[...]
    oc = jnp.concatenate(
        [_rms(oc[:, hd * C_HEAD_V:(hd + 1) * C_HEAD_V], cg) for hd in range(C_HEADS)], axis=1)
    o_c = oc * jax.nn.silu(xg)

    def gate(i):
        off = OFF_GATES + i * D_MODEL
        return jax.nn.sigmoid(_dot(h, win_ref[:, off:off + D_MODEL]))

    merged = gate(0) * _dot(oa_s[...].astype(BF16), wbr_ref[0:A_WIDTH, :])
    merged = merged + gate(1) * _dot(o_b.astype(BF16), wbr_ref[A_WIDTH:A_WIDTH + B_WIDTH, :])
    merged = merged + gate(2) * _dot(o_c.astype(BF16), wbr_ref[A_WIDTH + B_WIDTH:, :])
    y = _dot(merged.astype(BF16), wout_ref[...])
    xo_ref[...] = (x + _rms(y, g_ref[3:4, :])).reshape(n_seq, seq_tile, D_MODEL)

    @pl.when(t_idx == pl.num_programs(1) - 1)
    def _final():
        for s in range(n_seq):
            for hd in range(C_HEADS):
                so_ref[s, hd] = st_s[s, hd].T


def _mixer(x, pool0, s0, gains, w_in, vgain, w_s, b_sT, w_pool, pscale, lb_raw, cgain, w_br, w_out,
           layer, n_seq, seq_tile, n_valid, emit_v):
    bsz, t, _ = x.shape
    chunk = min(seq_tile, GMLP_CHUNK)
    rows = n_seq * seq_tile
    grid = (bsz // n_seq, t // seq_tile)
    const = dict(pipeline_mode=pl.Buffered(1))

    def lsel(*tail):
        return lambda b, i: (layer,) + tail

    in_specs = [
        pl.BlockSpec((n_seq, seq_tile, D_MODEL), lambda b, i: (b, i, 0)),
        pl.BlockSpec((n_seq, POOL_BUF, B_WIDTH), lambda b, i: (b, 0, 0)),
        pl.BlockSpec((n_seq, C_HEADS, C_HEAD_K, C_HEAD_V), lambda b, i: (b, 0, 0, 0)),
        pl.BlockSpec((None, 6, D_MODEL), lsel(0, 0), **const),
        pl.BlockSpec((None, D_MODEL, IN_COLS), lsel(0, 0), **const),
        pl.BlockSpec((None, 1, A_WIDTH), lsel(0, 0), **const),
        pl.BlockSpec((None, A_HEADS, GMLP_CHUNK, GMLP_CHUNK), lsel(0, 0, 0), **const),
        pl.BlockSpec((None, GMLP_CHUNK, A_HEADS), lsel(0, 0), **const),
        pl.BlockSpec((None, B_GROUPS, B_GROUP, B_GROUP), lsel(0, 0, 0), **const),
        pl.BlockSpec((None, 1, B_WIDTH), lsel(0, 0), **const),
        pl.BlockSpec((DEPTH, C_WIDTH), lambda b, i: (0, 0), **const),
        pl.BlockSpec((None, 1, C_HEAD_V), lsel(0, 0), **const),
        pl.BlockSpec((None, N_BRANCH * A_WIDTH, D_MODEL), lsel(0, 0), **const),
        pl.BlockSpec((None, D_MODEL, D_MODEL), lsel(0, 0), **const),
    ]
    out_specs = [
        pl.BlockSpec((n_seq, seq_tile, D_MODEL), lambda b, i: (b, i, 0)),
        pl.BlockSpec((n_seq, POOL_BUF, B_WIDTH), lambda b, i: (b, 0, 0)),
        pl.BlockSpec((n_seq, C_HEADS, C_HEAD_K, C_HEAD_V), lambda b, i: (b, 0, 0, 0)),
    ]
    out_shape = [
        jax.ShapeDtypeStruct(x.shape, F32),
        jax.ShapeDtypeStruct((bsz, POOL_BUF, B_WIDTH), F32),
        jax.ShapeDtypeStruct((bsz, C_HEADS, C_HEAD_K, C_HEAD_V), F32),
    ]
    if emit_v:
        out_specs.append(pl.BlockSpec((n_seq, seq_tile, A_WIDTH), lambda b, i: (b, i, 0)))
        out_shape.append(jax.ShapeDtypeStruct((bsz, t, A_WIDTH), F32))
    act = pltpu.VMEM((rows, C_WIDTH), F32)
    scratch = [act] * 8 + [
        pltpu.VMEM((n_seq, C_HEADS, C_HEAD_V, C_HEAD_K), F32),
        pltpu.VMEM((n_seq, POOL_PAD + seq_tile, B_WIDTH), F32),
    ]
    return pl.pallas_call(
        functools.partial(_mixer_kernel, layer=layer, n_seq=n_seq, seq_tile=seq_tile, chunk=chunk,
                          n_valid=n_valid, emit_v=emit_v),
        grid=grid,
        in_specs=in_specs,
        out_specs=out_specs,
        out_shape=out_shape,
        scratch_shapes=scratch,
        compiler_params=pltpu.CompilerParams(
            dimension_semantics=("arbitrary", "arbitrary"), vmem_limit_bytes=VMEM_LIMIT),
        name=f"mixer_l{layer}_{'dec' if emit_v else 'pre'}",
    )(x, pool0, s0, gains, w_in, vgain, w_s, b_sT, w_pool, pscale, lb_raw, cgain, w_br, w_out)


def kernel(x_prompt, x_sample, state_pool, state_hgrn, norm_gains, w_ffn1_gu, w_ffn1_down, w_in,
           gmlp_v_gain, gmlp_w_s, gmlp_b_s, pool_w, pool_scale, hgrn_lower_bounds, hgrn_out_gain,
           w_branch, w_out, w_ffn2_gu, w_ffn2_down):
    bp, tp, _ = x_prompt.shape
    bs, ts, _ = x_sample.shape
    w1gu, w1dn = w_ffn1_gu.astype(BF16), w_ffn1_down.astype(BF16)
    w2gu, w2dn = w_ffn2_gu.astype(BF16), w_ffn2_down.astype(BF16)
    win, wbr, wout = w_in.astype(BF16), w_branch.astype(BF16), w_out.astype(BF16)
    vgain = gmlp_v_gain.reshape(DEPTH, 1, A_WIDTH)
    pscale = pool_scale.reshape(DEPTH, 1, B_WIDTH)
    cgain = hgrn_out_gain.reshape(DEPTH, 1, C_HEAD_V)
    b_sT = jnp.swapaxes(gmlp_b_s, 1, 2)
    zero_pool = jnp.zeros((bp, POOL_BUF, B_WIDTH), F32)
    zero_state = jnp.zeros((bp, C_HEADS, C_HEAD_K, C_HEAD_V), F32)

    xp, xs = x_prompt, x_sample
    pool_p, pool_s, hg_p, hg_s, v_s = [], [], [], [], []
    for l in range(DEPTH):
        shared = (norm_gains, win, vgain, gmlp_w_s, b_sT, pool_w, pscale, hgrn_lower_bounds, cgain,
                  wbr, wout)
        xp = _ffn(xp.reshape(bp * tp, D_MODEL), norm_gains, w1gu, w1dn, l, 0, 1).reshape(bp, tp, D_MODEL)
        xs = _ffn(xs.reshape(bs * ts, D_MODEL), norm_gains, w1gu, w1dn, l, 0, 1).reshape(bs, ts, D_MODEL)
        xp, pp, sp = _mixer(xp, zero_pool, zero_state, *shared, layer=l, n_seq=1,
                            seq_tile=min(MIX_ROWS, tp), n_valid=0, emit_v=False)
        xs, ps, ss, vs = _mixer(xs, state_pool[l], state_hgrn[l], *shared, layer=l, n_seq=bs,
                                seq_tile=ts, n_valid=POOL_BUF, emit_v=True)
        xp = _ffn(xp.reshape(bp * tp, D_MODEL), norm_gains, w2gu, w2dn, l, 4, 5).reshape(bp, tp, D_MODEL)
        xs = _ffn(xs.reshape(bs * ts, D_MODEL), norm_gains, w2gu, w2dn, l, 4, 5).reshape(bs, ts, D_MODEL)
        pool_p.append(pp)
        pool_s.append(ps)
        hg_p.append(sp)
        hg_s.append(ss)
        v_s.append(vs)
    return (xp, xs, jnp.stack(pool_p), jnp.stack(pool_s), jnp.stack(hg_p), jnp.stack(hg_s),
            jnp.stack(v_s))
```

```python
import functools

import jax
import jax.numpy as jnp
from jax import lax
from jax.experimental import pallas as pl
from jax.experimental.pallas import tpu as pltpu

D_MODEL = 1024
DEPTH = 4
D_FF = 2816
A_HEADS = 4
A_HEAD = 128
A_WIDTH = A_HEADS * A_HEAD
GMLP_CHUNK = 128
B_GROUPS = 4
B_GROUP = 128
B_WIDTH = B_GROUPS * B_GROUP
POOL_WINDOWS = (2, 4, 8, 16)
POOL_BUF = 15
POOL_PAD = 16
C_HEADS = 4
C_HEAD_K = 128
C_HEAD_V = 128
C_WIDTH = C_HEADS * C_HEAD_V
N_BRANCH = 3
EPS = 1e-6
F_FLOOR = 1e-30
LOG2E = 1.4426950408889634

OFF_UV = 0
OFF_POOL = 2 * A_WIDTH
OFF_HGRN = OFF_POOL + B_WIDTH
OFF_GATES = OFF_HGRN + 4 * C_WIDTH
IN_COLS = OFF_GATES + N_BRANCH * D_MODEL

SUBLANES = 8
FFN_ROWS = 512
FFN_SUBTILES = 2
MIX_ROWS = 256
GATE_PIECE = 256
VMEM_LIMIT = 56 * 1024 * 1024

F32 = jnp.float32
BF16 = jnp.bfloat16


def _rms(x, g):
    ms = jnp.mean(x * x, axis=-1, keepdims=True)
    return x * lax.rsqrt(ms + EPS) * g


def _dot(a, b):
    return jnp.dot(a, b, preferred_element_type=F32)


def _dot_nt(a, b):
    return lax.dot_general(a, b, (((1,), (1,)), ((), ())), preferred_element_type=F32)


def _dot_tn(a, b):
    return lax.dot_general(a, b, (((0,), (0,)), ((), ())), preferred_element_type=F32)


def _ffn_kernel(x_ref, g_ref, wgu_ref, wdn_ref, o_ref, *, pre, post, n_sub):
    sub = x_ref.shape[0] // n_sub
    for s in range(n_sub):
        rs = slice(s * sub, (s + 1) * sub)
        x = x_ref[rs, :]
        h = _rms(x, g_ref[pre:pre + 1, :]).astype(BF16)
        g = _dot(h, wgu_ref[:, :D_FF])
        u = _dot(h, wgu_ref[:, D_FF:])
        a = (jax.nn.silu(g) * u).astype(BF16)
        y = _dot(a, wdn_ref[...])
        o_ref[rs, :] = x + 0.5 * _rms(y, g_ref[post:post + 1, :])


def _ffn(x2d, gains, wgu, wdn, layer, pre, post):
    rows = x2d.shape[0]
    tm = min(FFN_ROWS, rows)
    const = dict(pipeline_mode=pl.Buffered(1))
    return pl.pallas_call(
        functools.partial(_ffn_kernel, pre=pre, post=post, n_sub=FFN_SUBTILES),
        grid=(rows // tm,),
        in_specs=[
            pl.BlockSpec((tm, D_MODEL), lambda i: (i, 0)),
            pl.BlockSpec((None, 6, D_MODEL), lambda i: (layer, 0, 0), **const),
            pl.BlockSpec((None, D_MODEL, 2 * D_FF), lambda i: (layer, 0, 0), **const),
            pl.BlockSpec((None, D_FF, D_MODEL), lambda i: (layer, 0, 0), **const),
        ],
        out_specs=pl.BlockSpec((tm, D_MODEL), lambda i: (i, 0)),
        out_shape=jax.ShapeDtypeStruct(x2d.shape, F32),
        compiler_params=pltpu.CompilerParams(
            dimension_semantics=("arbitrary",), vmem_limit_bytes=VMEM_LIMIT),
        name=f"ffn_l{layer}_{pre}",
    )(x2d, gains, wgu, wdn)


def _cumsum_rows(x):
    n = x.shape[0]
    row = lax.broadcasted_iota(jnp.int32, x.shape, 0)
    s = 1
    while s < n:
        x = x + jnp.where(row >= s, pltpu.roll(x, s, axis=0), 0.0)
        s *= 2
    return x


def _block_row(x, blk, r):
    n, w = x.shape
    if blk >= SUBLANES:
        y = x.reshape(n // blk, blk, w)[:, r:r + 1, :]
        return jnp.broadcast_to(y, (n // blk, blk, w)).reshape(n, w)
    row = lax.broadcasted_iota(jnp.int32, x.shape, 0) & (SUBLANES - 1)
    if blk == 4:
        assert r == 1
        return jnp.where(row < 4, _block_row(x, SUBLANES, 1), _block_row(x, SUBLANES, 5))
    assert blk == 2 and r == 0
    return jnp.where((row & 1) == 1, pltpu.roll(x, 1, axis=0), x)


def _hgrn_chunk(q, k, logf, v, st_ref, seq, fill):
    n = q.shape[0]
    row = lax.broadcasted_iota(jnp.int32, (n, C_WIDTH), 0)
    rr = lax.broadcasted_iota(jnp.int32, (n, n), 0)
    cc = lax.broadcasted_iota(jnp.int32, (n, n), 1)
    xr = rr ^ cc
    lower = rr > cc
    heads = [slice(hd * C_HEAD_K, (hd + 1) * C_HEAD_K) for hd in range(C_HEADS)]
    b = _cumsum_rows(logf)
    p0 = q * k
    p1 = q * jnp.exp(logf) * pltpu.roll(k, 1, axis=0)
    scores = []
    for sl in heads:
        d0 = jnp.sum(p0[:, sl], axis=-1, keepdims=True)
        d1 = jnp.sum(p1[:, sl], axis=-1, keepdims=True)
        scores.append(jnp.where(rr == cc, d0, jnp.where((xr == 1) & lower, d1, 0.0)))
    fill()
    h = n // 2
    while h >= 2:
        bmid = _block_row(b, 2 * h, h - 1)
        m = jnp.where((row & h) != 0, q, k) * jnp.exp2(jnp.abs(b - bmid) * (-LOG2E))
        m = m.astype(BF16)
        pair = ((xr & -h) == h) & lower
        for hd, sl in enumerate(heads):
            scores[hd] = jnp.where(pair, _dot_nt(m[:, sl], m[:, sl]), scores[hd])
        fill()
        h //= 2
    vb = v.astype(BF16)
    b_last = b[n - 1:n, :]
    qb = (q * jnp.exp(b)).astype(BF16)
    kb = (k * jnp.exp(b_last - b)).astype(BF16)
    decay = jnp.exp(b_last)
    outs = []
    for hd, sl in enumerate(heads):
        st = st_ref[seq, hd]
        outs.append(_dot(scores[hd].astype(BF16), vb[:, sl]) + _dot_nt(qb[:, sl], st.astype(BF16)))
        st_ref[seq, hd] = decay[:, sl] * st + _dot_tn(vb[:, sl], kb[:, sl])
    return jnp.concatenate(outs, axis=1)


def _mixer_kernel(x_ref, pool0_ref, s0_ref, g_ref, win_ref, vgain_ref, ws_ref, bs_ref, wpool_ref,
                  pscale_ref, lbraw_ref, cgain_ref, wbr_ref, wout_ref,
                  xo_ref, poolo_ref, so_ref, *rest,
                  layer, n_seq, seq_tile, chunk, n_valid, emit_v):
    if emit_v:
        vo_ref, rest = rest[0], rest[1:]
    u_s, v_s, q_s, k_s, lf_s, vi_s, oa_s, oc_s, g_s, st_s, ext_s = rest
    rows = n_seq * seq_tile
    t_idx = pl.program_id(1)

    @pl.when(t_idx == 0)
    def _init():
        for s in range(n_seq):
            for hd in range(C_HEADS):
                st_s[s, hd] = s0_ref[s, hd].T
            ext_s[s, 0:1, :] = jnp.zeros((1, B_WIDTH), F32)
            ext_s[s, POOL_PAD - POOL_BUF:POOL_PAD, :] = pool0_ref[s]

    x = x_ref[...].reshape(rows, D_MODEL)
    h = _rms(x, g_ref[2:3, :]).astype(BF16)

    def proj(off, width):
        return _dot(h, win_ref[:, off:off + width])

    pa = proj(OFF_UV, 2 * A_WIDTH)
    xb = proj(OFF_POOL, B_WIDTH)

    u_s[...] = jax.nn.gelu(pa[:, :A_WIDTH])
    v = _rms(jax.nn.gelu(pa[:, A_WIDTH:]), vgain_ref[...])
    v_s[...] = v
    if emit_v:
        vo_ref[...] = v.reshape(n_seq, seq_tile, A_WIDTH)

    pq_qf = proj(OFF_HGRN, 2 * C_WIDTH)

    t_loc = lax.broadcasted_iota(jnp.int32, (seq_tile, 1), 0)
    avail = t_idx * seq_tile + t_loc + (1 + n_valid)
    ob_rows = []
    for s in range(n_seq):
        ext_s[s, POOL_PAD:POOL_PAD + seq_tile, :] = xb[s * seq_tile:(s + 1) * seq_tile]
        parts = []
        for g, w in enumerate(POOL_WINDOWS):
            sl = slice(g * B_GROUP, (g + 1) * B_GROUP)
            x0 = ext_s[s, POOL_PAD:POOL_PAD + seq_tile, sl]
            acc = x0
            for j in range(1, w):
                acc = acc + ext_s[s, POOL_PAD - j:POOL_PAD - j + seq_tile, sl]
            inv = 1.0 / jnp.minimum(avail, w).astype(F32)
            pooled = acc * inv - x0
            parts.append(_dot(pooled.astype(BF16), wpool_ref[g].astype(BF16)))
        ob_rows.append(jnp.concatenate(parts, axis=1))
        poolo_ref[s] = ext_s[s, seq_tile + POOL_PAD - POOL_BUF:seq_tile + POOL_PAD, :]
        ext_s[s, 0:POOL_PAD, :] = ext_s[s, seq_tile:seq_tile + POOL_PAD, :]
    o_b = jnp.concatenate(ob_rows, axis=0) * pscale_ref[...]

    pq_ig = proj(OFF_HGRN + 2 * C_WIDTH, 2 * C_WIDTH)

    lbr = lbraw_ref[...]
    e = jnp.exp(lbr - jnp.max(lbr, axis=0, keepdims=True))
    soft = e / jnp.sum(e, axis=0, keepdims=True)
    lb = jnp.zeros((1, C_WIDTH), F32)
    for j in range(1, layer + 1):
        lb = lb + soft[j:j + 1, :]
    q_s[...] = jax.nn.silu(pq_qf[:, :C_WIDTH]) * (C_HEAD_K ** -0.5)
    z = pq_qf[:, C_WIDTH:]
    f = lb + (1.0 - lb) * jax.nn.sigmoid(z)
    lf_s[...] = jnp.log(jnp.maximum(f, F_FLOOR))
    k_s[...] = (1.0 - lb) * jax.nn.sigmoid(-z)
    vi_s[...] = pq_ig[:, :C_WIDTH]
    xg = pq_ig[:, C_WIDTH:]
    y_b = _dot(o_b.astype(BF16), wbr_ref[A_WIDTH:A_WIDTH + B_WIDTH, :])

    tri = (lax.broadcasted_iota(jnp.int32, (chunk, chunk), 0)
           >= lax.broadcasted_iota(jnp.int32, (chunk, chunk), 1))
    w_mix = [jnp.where(tri, ws_ref[hd, 0:chunk, 0:chunk], 0.0).astype(BF16) for hd in range(A_HEADS)]
    chunks_per_seq = seq_tile // chunk

    def gate_piece(p):
        cols = slice(p * GATE_PIECE, (p + 1) * GATE_PIECE)
        g_s[:, cols] = proj(OFF_GATES + p * GATE_PIECE, GATE_PIECE)

    pieces = iter(range(N_BRANCH * D_MODEL // GATE_PIECE))

    def fill():
        p = next(pieces, None)
        if p is not None:
            gate_piece(p)

    for c in range(rows // chunk):
        rs = slice(c * chunk, (c + 1) * chunk)
        vb = v_s[rs, :].astype(BF16)
        for hd in range(A_HEADS):
            sl = slice(hd * A_HEAD, (hd + 1) * A_HEAD)
            mixed = _dot(w_mix[hd], vb[:, sl]) + bs_ref[0:chunk, hd:hd + 1]
            oa_s[rs, sl] = u_s[rs, sl] * mixed
        oc_s[rs, :] = _hgrn_chunk(q_s[rs, :], k_s[rs, :], lf_s[rs, :], vi_s[rs, :], st_s,
                                  c // chunks_per_seq, fill)
    for p in pieces:
        gate_piece(p)

    y_a = _dot(oa_s[...].astype(BF16), wbr_ref[0:A_WIDTH, :])
    oc = oc_s[...]
    cg = cgain_ref[...]
    oc = jnp.concatenate(
        [_rms(oc[:, hd * C_HEAD_V:(hd + 1) * C_HEAD_V], cg) for hd in range(C_HEADS)], axis=1)
    o_c = oc * jax.nn.silu(xg)
    y_c = _dot(o_c.astype(BF16), wbr_ref[A_WIDTH + B_WIDTH:, :])
    merged = jax.nn.sigmoid(g_s[:, 0:D_MODEL]) * y_a
    merged = merged + jax.nn.sigmoid(g_s[:, D_MODEL:2 * D_MODEL]) * y_b
    merged = merged + jax.nn.sigmoid(g_s[:, 2 * D_MODEL:]) * y_c
    y = _dot(merged.astype(BF16), wout_ref[...])
    xo_ref[...] = (x + _rms(y, g_ref[3:4, :])).reshape(n_seq, seq_tile, D_MODEL)

    @pl.when(t_idx == pl.num_programs(1) - 1)
    def _final():
        for s in range(n_seq):
            for hd in range(C_HEADS):
                so_ref[s, hd] = st_s[s, hd].T


def _mixer(x, pool0, s0, gains, w_in, vgain, w_s, b_sT, w_pool, pscale, lb_raw, cgain, w_br, w_out,
           layer, n_seq, seq_tile, n_valid, emit_v):
    bsz, t, _ = x.shape
    chunk = min(seq_tile, GMLP_CHUNK)
    rows = n_seq * seq_tile
    grid = (bsz // n_seq, t // seq_tile)
    const = dict(pipeline_mode=pl.Buffered(1))

    def lsel(*tail):
        return lambda b, i: (layer,) + tail

    in_specs = [
        pl.BlockSpec((n_seq, seq_tile, D_MODEL), lambda b, i: (b, i, 0)),
        pl.BlockSpec((n_seq, POOL_BUF, B_WIDTH), lambda b, i: (b, 0, 0)),
        pl.BlockSpec((n_seq, C_HEADS, C_HEAD_K, C_HEAD_V), lambda b, i: (b, 0, 0, 0)),
        pl.BlockSpec((None, 6, D_MODEL), lsel(0, 0), **const),
        pl.BlockSpec((None, D_MODEL, IN_COLS), lsel(0, 0), **const),
        pl.BlockSpec((None, 1, A_WIDTH), lsel(0, 0), **const),
        pl.BlockSpec((None, A_HEADS, GMLP_CHUNK, GMLP_CHUNK), lsel(0, 0, 0), **const),
        pl.BlockSpec((None, GMLP_CHUNK, A_HEADS), lsel(0, 0), **const),
        pl.BlockSpec((None, B_GROUPS, B_GROUP, B_GROUP), lsel(0, 0, 0), **const),
        pl.BlockSpec((None, 1, B_WIDTH), lsel(0, 0), **const),
        pl.BlockSpec((DEPTH, C_WIDTH), lambda b, i: (0, 0), **const),
        pl.BlockSpec((None, 1, C_HEAD_V), lsel(0, 0), **const),
        pl.BlockSpec((None, N_BRANCH * A_WIDTH, D_MODEL), lsel(0, 0), **const),
        pl.BlockSpec((None, D_MODEL, D_MODEL), lsel(0, 0), **const),
    ]
    out_specs = [
        pl.BlockSpec((n_seq, seq_tile, D_MODEL), lambda b, i: (b, i, 0)),
        pl.BlockSpec((n_seq, POOL_BUF, B_WIDTH), lambda b, i: (b, 0, 0)),
        pl.BlockSpec((n_seq, C_HEADS, C_HEAD_K, C_HEAD_V), lambda b, i: (b, 0, 0, 0)),
    ]
    out_shape = [
        jax.ShapeDtypeStruct(x.shape, F32),
        jax.ShapeDtypeStruct((bsz, POOL_BUF, B_WIDTH), F32),
        jax.ShapeDtypeStruct((bsz, C_HEADS, C_HEAD_K, C_HEAD_V), F32),
    ]
    if emit_v:
        out_specs.append(pl.BlockSpec((n_seq, seq_tile, A_WIDTH), lambda b, i: (b, i, 0)))
        out_shape.append(jax.ShapeDtypeStruct((bsz, t, A_WIDTH), F32))
    act = pltpu.VMEM((rows, C_WIDTH), F32)
    scratch = [act] * 8 + [
        pltpu.VMEM((rows, N_BRANCH * D_MODEL), F32),
        pltpu.VMEM((n_seq, C_HEADS, C_HEAD_V, C_HEAD_K), F32),
        pltpu.VMEM((n_seq, POOL_PAD + seq_tile, B_WIDTH), F32),
    ]
    return pl.pallas_call(
        functools.partial(_mixer_kernel, layer=layer, n_seq=n_seq, seq_tile=seq_tile, chunk=chunk,
                          n_valid=n_valid, emit_v=emit_v),
        grid=grid,
        in_specs=in_specs,
        out_specs=out_specs,
        out_shape=out_shape,
        scratch_shapes=scratch,
        compiler_params=pltpu.CompilerParams(
            dimension_semantics=("arbitrary", "arbitrary"), vmem_limit_bytes=VMEM_LIMIT),
        name=f"mixer_l{layer}_{'dec' if emit_v else 'pre'}",
    )(x, pool0, s0, gains, w_in, vgain, w_s, b_sT, w_pool, pscale, lb_raw, cgain, w_br, w_out)


def kernel(x_prompt, x_sample, state_pool, state_hgrn, norm_gains, w_ffn1_gu, w_ffn1_down, w_in,
           gmlp_v_gain, gmlp_w_s, gmlp_b_s, pool_w, pool_scale, hgrn_lower_bounds, hgrn_out_gain,
           w_branch, w_out, w_ffn2_gu, w_ffn2_down):
    bp, tp, _ = x_prompt.shape
    bs, ts, _ = x_sample.shape
    w1gu, w1dn = w_ffn1_gu.astype(BF16), w_ffn1_down.astype(BF16)
    w2gu, w2dn = w_ffn2_gu.astype(BF16), w_ffn2_down.astype(BF16)
    win, wbr, wout = w_in.astype(BF16), w_branch.astype(BF16), w_out.astype(BF16)
    vgain = gmlp_v_gain.reshape(DEPTH, 1, A_WIDTH)
    pscale = pool_scale.reshape(DEPTH, 1, B_WIDTH)
    cgain = hgrn_out_gain.reshape(DEPTH, 1, C_HEAD_V)
    b_sT = jnp.swapaxes(gmlp_b_s, 1, 2)
    zero_pool = jnp.zeros((bp, POOL_BUF, B_WIDTH), F32)
    zero_state = jnp.zeros((bp, C_HEADS, C_HEAD_K, C_HEAD_V), F32)

    xp, xs = x_prompt, x_sample
    pool_p, pool_s, hg_p, hg_s, v_s = [], [], [], [], []
    for l in range(DEPTH):
        shared = (norm_gains, win, vgain, gmlp_w_s, b_sT, pool_w, pscale, hgrn_lower_bounds, cgain,
                  wbr, wout)
        xp = _ffn(xp.reshape(bp * tp, D_MODEL), norm_gains, w1gu, w1dn, l, 0, 1).reshape(bp, tp, D_MODEL)
        xs = _ffn(xs.reshape(bs * ts, D_MODEL), norm_gains, w1gu, w1dn, l, 0, 1).reshape(bs, ts, D_MODEL)
        xp, pp, sp = _mixer(xp, zero_pool, zero_state, *shared, layer=l, n_seq=1,
                            seq_tile=min(MIX_ROWS, tp), n_valid=0, emit_v=False)
        xs, ps, ss, vs = _mixer(xs, state_pool[l], state_hgrn[l], *shared, layer=l, n_seq=bs,
                                seq_tile=ts, n_valid=POOL_BUF, emit_v=True)
        xp = _ffn(xp.reshape(bp * tp, D_MODEL), norm_gains, w2gu, w2dn, l, 4, 5).reshape(bp, tp, D_MODEL)
        xs = _ffn(xs.reshape(bs * ts, D_MODEL), norm_gains, w2gu, w2dn, l, 4, 5).reshape(bs, ts, D_MODEL)
        pool_p.append(pp)
        pool_s.append(ps)
        hg_p.append(sp)
        hg_s.append(ss)
        v_s.append(vs)
    return (xp, xs, jnp.stack(pool_p), jnp.stack(pool_s), jnp.stack(hg_p), jnp.stack(hg_s),
            jnp.stack(v_s))
```

```python
import functools

import jax
import jax.numpy as jnp
from jax import lax
from jax.experimental import pallas as pl
from jax.experimental.pallas import tpu as pltpu

D_MODEL = 1024
DEPTH = 4
D_FF = 2816
A_HEADS = 4
A_HEAD = 128
A_WIDTH = A_HEADS * A_HEAD
GMLP_CHUNK = 128
B_GROUPS = 4
B_GROUP = 128
B_WIDTH = B_GROUPS * B_GROUP
POOL_WINDOWS = (2, 4, 8, 16)
POOL_BUF = 15
POOL_PAD = 16
C_HEADS = 4
C_HEAD_K = 128
C_HEAD_V = 128
C_WIDTH = C_HEADS * C_HEAD_V
N_BRANCH = 3
EPS = 1e-6
F_FLOOR = 1e-30
LOG2E = 1.4426950408889634

OFF_UV = 0
OFF_POOL = 2 * A_WIDTH
OFF_HGRN = OFF_POOL + B_WIDTH
OFF_GATES = OFF_HGRN + 4 * C_WIDTH
IN_COLS = OFF_GATES + N_BRANCH * D_MODEL

SUBLANES = 8
FFN_ROWS = 1024
FFN_SUB_ROWS = 256
MIX_ROWS = 512
GATE_PIECE = 256
VMEM_LIMIT = 56 * 1024 * 1024

F32 = jnp.float32
BF16 = jnp.bfloat16


def _rms(x, g):
    ms = jnp.mean(x * x, axis=-1, keepdims=True)
    return x * lax.rsqrt(ms + EPS) * g


def _dot(a, b):
    return jnp.dot(a, b, preferred_element_type=F32)


def _dot_nt(a, b):
    return lax.dot_general(a, b, (((1,), (1,)), ((), ())), preferred_element_type=F32)


def _dot_tn(a, b):
    return lax.dot_general(a, b, (((0,), (0,)), ((), ())), preferred_element_type=F32)


def _ffn_kernel(x_ref, g_ref, wgu_ref, wdn_ref, o_ref, *, pre, post, n_sub):
    sub = x_ref.shape[0] // n_sub
    for s in range(n_sub):
        rs = slice(s * sub, (s + 1) * sub)
        x = x_ref[rs, :]
        h = _rms(x, g_ref[pre:pre + 1, :]).astype(BF16)
        g = _dot(h, wgu_ref[:, :D_FF])
        u = _dot(h, wgu_ref[:, D_FF:])
        a = (jax.nn.silu(g) * u).astype(BF16)
        y = _dot(a, wdn_ref[...])
        o_ref[rs, :] = x + 0.5 * _rms(y, g_ref[post:post + 1, :])


def _ffn(x2d, gains, wgu, wdn, layer, pre, post):
    rows = x2d.shape[0]
    tm = min(FFN_ROWS, rows)
    const = dict(pipeline_mode=pl.Buffered(1))
    return pl.pallas_call(
        functools.partial(_ffn_kernel, pre=pre, post=post, n_sub=max(1, tm // FFN_SUB_ROWS)),
        grid=(rows // tm,),
        in_specs=[
            pl.BlockSpec((tm, D_MODEL), lambda i: (i, 0)),
            pl.BlockSpec((None, 6, D_MODEL), lambda i: (layer, 0, 0), **const),
            pl.BlockSpec((None, D_MODEL, 2 * D_FF), lambda i: (layer, 0, 0), **const),
            pl.BlockSpec((None, D_FF, D_MODEL), lambda i: (layer, 0, 0), **const),
        ],
        out_specs=pl.BlockSpec((tm, D_MODEL), lambda i: (i, 0)),
        out_shape=jax.ShapeDtypeStruct(x2d.shape, F32),
        compiler_params=pltpu.CompilerParams(
            dimension_semantics=("arbitrary",), vmem_limit_bytes=VMEM_LIMIT),
        name=f"ffn_l{layer}_{pre}",
    )(x2d, gains, wgu, wdn)


def _cumsum_rows(x):
    n = x.shape[0]
    row = lax.broadcasted_iota(jnp.int32, x.shape, 0)
    s = 1
    while s < n:
        x = x + jnp.where(row >= s, pltpu.roll(x, s, axis=0), 0.0)
        s *= 2
    return x


def _block_row(x, blk, r):
    n, w = x.shape
    if blk >= SUBLANES:
        y = x.reshape(n // blk, blk, w)[:, r:r + 1, :]
        return jnp.broadcast_to(y, (n // blk, blk, w)).reshape(n, w)
    row = lax.broadcasted_iota(jnp.int32, x.shape, 0) & (SUBLANES - 1)
    if blk == 4:
        assert r == 1
        return jnp.where(row < 4, _block_row(x, SUBLANES, 1), _block_row(x, SUBLANES, 5))
    assert blk == 2 and r == 0
    return jnp.where((row & 1) == 1, pltpu.roll(x, 1, axis=0), x)


def _hgrn_chunk(q, k, logf, v, st_ref, seq, fill):
    n = q.shape[0]
    row = lax.broadcasted_iota(jnp.int32, (n, C_WIDTH), 0)
    rr = lax.broadcasted_iota(jnp.int32, (n, n), 0)
    cc = lax.broadcasted_iota(jnp.int32, (n, n), 1)
    xr = rr ^ cc
    lower = rr > cc
    heads = [slice(hd * C_HEAD_K, (hd + 1) * C_HEAD_K) for hd in range(C_HEADS)]
    b = _cumsum_rows(logf)
    p0 = q * k
    p1 = q * jnp.exp(logf) * pltpu.roll(k, 1, axis=0)
    scores = []
    for sl in heads:
        d0 = jnp.sum(p0[:, sl], axis=-1, keepdims=True)
        d1 = jnp.sum(p1[:, sl], axis=-1, keepdims=True)
        scores.append(jnp.where(rr == cc, d0, jnp.where((xr == 1) & lower, d1, 0.0)))
    fill()
    h = n // 2
    while h >= 2:
        bmid = _block_row(b, 2 * h, h - 1)
        m = jnp.where((row & h) != 0, q, k) * jnp.exp2(jnp.abs(b - bmid) * (-LOG2E))
        m = m.astype(BF16)
        pair = ((xr & -h) == h) & lower
        for hd, sl in enumerate(heads):
            scores[hd] = jnp.where(pair, _dot_nt(m[:, sl], m[:, sl]), scores[hd])
        fill()
        h //= 2
    vb = v.astype(BF16)
    b_last = b[n - 1:n, :]
    qb = (q * jnp.exp(b)).astype(BF16)
    kb = (k * jnp.exp(b_last - b)).astype(BF16)
    decay = jnp.exp(b_last)
    outs = []
    for hd, sl in enumerate(heads):
        st = st_ref[seq, hd]
        outs.append(_dot(scores[hd].astype(BF16), vb[:, sl]) + _dot_nt(qb[:, sl], st.astype(BF16)))
        st_ref[seq, hd] = decay[:, sl] * st + _dot_tn(vb[:, sl], kb[:, sl])
    return jnp.concatenate(outs, axis=1)


def _mixer_kernel(x_ref, pool0_ref, s0_ref, g_ref, win_ref, vgain_ref, ws_ref, bs_ref, wpool_ref,
                  pscale_ref, lbraw_ref, cgain_ref, wbr_ref, wout_ref,
                  xo_ref, poolo_ref, so_ref, *rest,
                  layer, n_seq, seq_tile, chunk, n_valid, emit_v):
    if emit_v:
        vo_ref, rest = rest[0], rest[1:]
    u_s, v_s, q_s, k_s, lf_s, vi_s, oa_s, oc_s, g_s, st_s, ext_s = rest
    rows = n_seq * seq_tile
    t_idx = pl.program_id(1)

    @pl.when(t_idx == 0)
    def _init():
        for s in range(n_seq):
            for hd in range(C_HEADS):
                st_s[s, hd] = s0_ref[s, hd].T
            ext_s[s, 0:1, :] = jnp.zeros((1, B_WIDTH), F32)
            ext_s[s, POOL_PAD - POOL_BUF:POOL_PAD, :] = pool0_ref[s]

    x = x_ref[...].reshape(rows, D_MODEL)
    h = _rms(x, g_ref[2:3, :]).astype(BF16)

    def proj(off, width):
        return _dot(h, win_ref[:, off:off + width])

    pa = proj(OFF_UV, 2 * A_WIDTH)
    xb = proj(OFF_POOL, B_WIDTH)

    u_s[...] = jax.nn.gelu(pa[:, :A_WIDTH])
    v = _rms(jax.nn.gelu(pa[:, A_WIDTH:]), vgain_ref[...])
    v_s[...] = v
    if emit_v:
        vo_ref[...] = v.reshape(n_seq, seq_tile, A_WIDTH)

    pq_qf = proj(OFF_HGRN, 2 * C_WIDTH)

    t_loc = lax.broadcasted_iota(jnp.int32, (seq_tile, 1), 0)
    avail = t_idx * seq_tile + t_loc + (1 + n_valid)
    ob_rows = []
    for s in range(n_seq):
        ext_s[s, POOL_PAD:POOL_PAD + seq_tile, :] = xb[s * seq_tile:(s + 1) * seq_tile]
        parts = []
        for g, w in enumerate(POOL_WINDOWS):
            sl = slice(g * B_GROUP, (g + 1) * B_GROUP)
            x0 = ext_s[s, POOL_PAD:POOL_PAD + seq_tile, sl]
            acc = x0
            for j in range(1, w):
                acc = acc + ext_s[s, POOL_PAD - j:POOL_PAD - j + seq_tile, sl]
            inv = 1.0 / jnp.minimum(avail, w).astype(F32)
            pooled = acc * inv - x0
            parts.append(_dot(pooled.astype(BF16), wpool_ref[g].astype(BF16)))
        ob_rows.append(jnp.concatenate(parts, axis=1))
        poolo_ref[s] = ext_s[s, seq_tile + POOL_PAD - POOL_BUF:seq_tile + POOL_PAD, :]
        ext_s[s, 0:POOL_PAD, :] = ext_s[s, seq_tile:seq_tile + POOL_PAD, :]
    o_b = jnp.concatenate(ob_rows, axis=0) * pscale_ref[...]

    pq_ig = proj(OFF_HGRN + 2 * C_WIDTH, 2 * C_WIDTH)

    lbr = lbraw_ref[...]
    e = jnp.exp(lbr - jnp.max(lbr, axis=0, keepdims=True))
    soft = e / jnp.sum(e, axis=0, keepdims=True)
    lb = jnp.zeros((1, C_WIDTH), F32)
    for j in range(1, layer + 1):
        lb = lb + soft[j:j + 1, :]
    q_s[...] = jax.nn.silu(pq_qf[:, :C_WIDTH]) * (C_HEAD_K ** -0.5)
    z = pq_qf[:, C_WIDTH:]
    f = lb + (1.0 - lb) * jax.nn.sigmoid(z)
    lf_s[...] = jnp.log(jnp.maximum(f, F_FLOOR))
    k_s[...] = (1.0 - lb) * jax.nn.sigmoid(-z)
    vi_s[...] = pq_ig[:, :C_WIDTH]
    xg = pq_ig[:, C_WIDTH:]
    y_b = _dot(o_b.astype(BF16), wbr_ref[A_WIDTH:A_WIDTH + B_WIDTH, :])

    tri = (lax.broadcasted_iota(jnp.int32, (chunk, chunk), 0)
           >= lax.broadcasted_iota(jnp.int32, (chunk, chunk), 1))
    w_mix = [jnp.where(tri, ws_ref[hd, 0:chunk, 0:chunk], 0.0).astype(BF16) for hd in range(A_HEADS)]
    chunks_per_seq = seq_tile // chunk

    def gate_piece(p):
        cols = slice(p * GATE_PIECE, (p + 1) * GATE_PIECE)
        g_s[:, cols] = proj(OFF_GATES + p * GATE_PIECE, GATE_PIECE)

    pieces = iter(range(N_BRANCH * D_MODEL // GATE_PIECE))

    def fill():
        p = next(pieces, None)
        if p is not None:
            gate_piece(p)

    for c in range(rows // chunk):
        rs = slice(c * chunk, (c + 1) * chunk)
        vb = v_s[rs, :].astype(BF16)
        for hd in range(A_HEADS):
            sl = slice(hd * A_HEAD, (hd + 1) * A_HEAD)
            mixed = _dot(w_mix[hd], vb[:, sl]) + bs_ref[0:chunk, hd:hd + 1]
            oa_s[rs, sl] = u_s[rs, sl] * mixed
        oc_s[rs, :] = _hgrn_chunk(q_s[rs, :], k_s[rs, :], lf_s[rs, :], vi_s[rs, :], st_s,
                                  c // chunks_per_seq, fill)
    for p in pieces:
        gate_piece(p)

    y_a = _dot(oa_s[...].astype(BF16), wbr_ref[0:A_WIDTH, :])
    oc = oc_s[...]
    cg = cgain_ref[...]
    oc = jnp.concatenate(
        [_rms(oc[:, hd * C_HEAD_V:(hd + 1) * C_HEAD_V], cg) for hd in range(C_HEADS)], axis=1)
    o_c = oc * jax.nn.silu(xg)
    y_c = _dot(o_c.astype(BF16), wbr_ref[A_WIDTH + B_WIDTH:, :])
    merged = jax.nn.sigmoid(g_s[:, 0:D_MODEL]) * y_a
    merged = merged + jax.nn.sigmoid(g_s[:, D_MODEL:2 * D_MODEL]) * y_b
    merged = merged + jax.nn.sigmoid(g_s[:, 2 * D_MODEL:]) * y_c
    y = _dot(merged.astype(BF16), wout_ref[...])
    xo_ref[...] = (x + _rms(y, g_ref[3:4, :])).reshape(n_seq, seq_tile, D_MODEL)

    @pl.when(t_idx == pl.num_programs(1) - 1)
    def _final():
        for s in range(n_seq):
            for hd in range(C_HEADS):
                so_ref[s, hd] = st_s[s, hd].T


def _mixer(x, pool0, s0, gains, w_in, vgain, w_s, b_sT, w_pool, pscale, lb_raw, cgain, w_br, w_out,
           layer, n_seq, seq_tile, n_valid, emit_v):
    bsz, t, _ = x.shape
    chunk = min(seq_tile, GMLP_CHUNK)
    rows = n_seq * seq_tile
    grid = (bsz // n_seq, t // seq_tile)
    const = dict(pipeline_mode=pl.Buffered(1))

    def lsel(*tail):
        return lambda b, i: (layer,) + tail

    in_specs = [
        pl.BlockSpec((n_seq, seq_tile, D_MODEL), lambda b, i: (b, i, 0)),
        pl.BlockSpec((n_seq, POOL_BUF, B_WIDTH), lambda b, i: (b, 0, 0)),
        pl.BlockSpec((n_seq, C_HEADS, C_HEAD_K, C_HEAD_V), lambda b, i: (b, 0, 0, 0)),
        pl.BlockSpec((None, 6, D_MODEL), lsel(0, 0), **const),
        pl.BlockSpec((None, D_MODEL, IN_COLS), lsel(0, 0), **const),
        pl.BlockSpec((None, 1, A_WIDTH), lsel(0, 0), **const),
        pl.BlockSpec((None, A_HEADS, GMLP_CHUNK, GMLP_CHUNK), lsel(0, 0, 0), **const),
        pl.BlockSpec((None, GMLP_CHUNK, A_HEADS), lsel(0, 0), **const),
        pl.BlockSpec((None, B_GROUPS, B_GROUP, B_GROUP), lsel(0, 0, 0), **const),
        pl.BlockSpec((None, 1, B_WIDTH), lsel(0, 0), **const),
        pl.BlockSpec((DEPTH, C_WIDTH), lambda b, i: (0, 0), **const),
        pl.BlockSpec((None, 1, C_HEAD_V), lsel(0, 0), **const),
        pl.BlockSpec((None, N_BRANCH * A_WIDTH, D_MODEL), lsel(0, 0), **const),
        pl.BlockSpec((None, D_MODEL, D_MODEL), lsel(0, 0), **const),
    ]
    out_specs = [
        pl.BlockSpec((n_seq, seq_tile, D_MODEL), lambda b, i: (b, i, 0)),
        pl.BlockSpec((n_seq, POOL_BUF, B_WIDTH), lambda b, i: (b, 0, 0)),
        pl.BlockSpec((n_seq, C_HEADS, C_HEAD_K, C_HEAD_V), lambda b, i: (b, 0, 0, 0)),
    ]
    out_shape = [
        jax.ShapeDtypeStruct(x.shape, F32),
        jax.ShapeDtypeStruct((bsz, POOL_BUF, B_WIDTH), F32),
        jax.ShapeDtypeStruct((bsz, C_HEADS, C_HEAD_K, C_HEAD_V), F32),
    ]
    if emit_v:
        out_specs.append(pl.BlockSpec((n_seq, seq_tile, A_WIDTH), lambda b, i: (b, i, 0)))
        out_shape.append(jax.ShapeDtypeStruct((bsz, t, A_WIDTH), F32))
    act = pltpu.VMEM((rows, C_WIDTH), F32)
    scratch = [act] * 8 + [
        pltpu.VMEM((rows, N_BRANCH * D_MODEL), F32),
        pltpu.VMEM((n_seq, C_HEADS, C_HEAD_V, C_HEAD_K), F32),
        pltpu.VMEM((n_seq, POOL_PAD + seq_tile, B_WIDTH), F32),
    ]
    return pl.pallas_call(
        functools.partial(_mixer_kernel, layer=layer, n_seq=n_seq, seq_tile=seq_tile, chunk=chunk,
                          n_valid=n_valid, emit_v=emit_v),
        grid=grid,
        in_specs=in_specs,
        out_specs=out_specs,
        out_shape=out_shape,
        scratch_shapes=scratch,
        compiler_params=pltpu.CompilerParams(
            dimension_semantics=("arbitrary", "arbitrary"), vmem_limit_bytes=VMEM_LIMIT),
        name=f"mixer_l{layer}_{'dec' if emit_v else 'pre'}",
    )(x, pool0, s0, gains, w_in, vgain, w_s, b_sT, w_pool, pscale, lb_raw, cgain, w_br, w_out)


def kernel(x_prompt, x_sample, state_pool, state_hgrn, norm_gains, w_ffn1_gu, w_ffn1_down, w_in,
           gmlp_v_gain, gmlp_w_s, gmlp_b_s, pool_w, pool_scale, hgrn_lower_bounds, hgrn_out_gain,
           w_branch, w_out, w_ffn2_gu, w_ffn2_down):
    bp, tp, _ = x_prompt.shape
    bs, ts, _ = x_sample.shape
    w1gu, w1dn = w_ffn1_gu.astype(BF16), w_ffn1_down.astype(BF16)
    w2gu, w2dn = w_ffn2_gu.astype(BF16), w_ffn2_down.astype(BF16)
    win, wbr, wout = w_in.astype(BF16), w_branch.astype(BF16), w_out.astype(BF16)
    vgain = gmlp_v_gain.reshape(DEPTH, 1, A_WIDTH)
    pscale = pool_scale.reshape(DEPTH, 1, B_WIDTH)
    cgain = hgrn_out_gain.reshape(DEPTH, 1, C_HEAD_V)
    b_sT = jnp.swapaxes(gmlp_b_s, 1, 2)
    zero_pool = jnp.zeros((bp, POOL_BUF, B_WIDTH), F32)
    zero_state = jnp.zeros((bp, C_HEADS, C_HEAD_K, C_HEAD_V), F32)

    xp, xs = x_prompt, x_sample
    pool_p, pool_s, hg_p, hg_s, v_s = [], [], [], [], []
    for l in range(DEPTH):
        shared = (norm_gains, win, vgain, gmlp_w_s, b_sT, pool_w, pscale, hgrn_lower_bounds, cgain,
                  wbr, wout)
        xp = _ffn(xp.reshape(bp * tp, D_MODEL), norm_gains, w1gu, w1dn, l, 0, 1).reshape(bp, tp, D_MODEL)
        xs = _ffn(xs.reshape(bs * ts, D_MODEL), norm_gains, w1gu, w1dn, l, 0, 1).reshape(bs, ts, D_MODEL)
        xp, pp, sp = _mixer(xp, zero_pool, zero_state, *shared, layer=l, n_seq=1,
                            seq_tile=min(MIX_ROWS, tp), n_valid=0, emit_v=False)
        xs, ps, ss, vs = _mixer(xs, state_pool[l], state_hgrn[l], *shared, layer=l, n_seq=bs,
                                seq_tile=ts, n_valid=POOL_BUF, emit_v=True)
        xp = _ffn(xp.reshape(bp * tp, D_MODEL), norm_gains, w2gu, w2dn, l, 4, 5).reshape(bp, tp, D_MODEL)
        xs = _ffn(xs.reshape(bs * ts, D_MODEL), norm_gains, w2gu, w2dn, l, 4, 5).reshape(bs, ts, D_MODEL)
        pool_p.append(pp)
        pool_s.append(ps)
        hg_p.append(sp)
        hg_s.append(ss)
        v_s.append(vs)
    return (xp, xs, jnp.stack(pool_p), jnp.stack(pool_s), jnp.stack(hg_p), jnp.stack(hg_s),
            jnp.stack(v_s))
```

```python
import functools

import jax
import jax.numpy as jnp
from jax import lax
from jax.experimental import pallas as pl
from jax.experimental.pallas import tpu as pltpu

D_MODEL = 1024
DEPTH = 4
D_FF = 2816
A_HEADS = 4
A_HEAD = 128
A_WIDTH = A_HEADS * A_HEAD
GMLP_CHUNK = 128
B_GROUPS = 4
B_GROUP = 128
B_WIDTH = B_GROUPS * B_GROUP
POOL_WINDOWS = (2, 4, 8, 16)
POOL_BUF = 15
POOL_PAD = 16
C_HEADS = 4
C_HEAD_K = 128
C_HEAD_V = 128
C_WIDTH = C_HEADS * C_HEAD_V
N_BRANCH = 3
EPS = 1e-6
F_FLOOR = 1e-30
LOG2E = 1.4426950408889634

OFF_UV = 0
OFF_POOL = 2 * A_WIDTH
OFF_HGRN = OFF_POOL + B_WIDTH
OFF_GATES = OFF_HGRN + 4 * C_WIDTH
IN_COLS = OFF_GATES + N_BRANCH * D_MODEL

SUBLANES = 8
FFN_ROWS = 1024
FFN_SUB_ROWS = 256
MIX_ROWS = 512
MIX_SUB_ROWS = 256
VMEM_LIMIT = 56 * 1024 * 1024

F32 = jnp.float32
BF16 = jnp.bfloat16


def _rms(x, g):
    ms = jnp.mean(x * x, axis=-1, keepdims=True)
    return x * lax.rsqrt(ms + EPS) * g


def _dot(a, b):
    return jnp.dot(a, b, preferred_element_type=F32)


def _dot_nt(a, b):
    return lax.dot_general(a, b, (((1,), (1,)), ((), ())), preferred_element_type=F32)


def _dot_tn(a, b):
    return lax.dot_general(a, b, (((0,), (0,)), ((), ())), preferred_element_type=F32)


def _ffn_kernel(x_ref, g_ref, wgu_ref, wdn_ref, o_ref, *, pre, post, n_sub):
    sub = x_ref.shape[0] // n_sub
    for s in range(n_sub):
        rs = slice(s * sub, (s + 1) * sub)
        x = x_ref[rs, :]
        h = _rms(x, g_ref[pre:pre + 1, :]).astype(BF16)
        g = _dot(h, wgu_ref[:, :D_FF])
        u = _dot(h, wgu_ref[:, D_FF:])
        a = (jax.nn.silu(g) * u).astype(BF16)
        y = _dot(a, wdn_ref[...])
        o_ref[rs, :] = x + 0.5 * _rms(y, g_ref[post:post + 1, :])


def _ffn(x2d, gains, wgu, wdn, layer, pre, post):
    rows = x2d.shape[0]
    tm = min(FFN_ROWS, rows)
    const = dict(pipeline_mode=pl.Buffered(1))
    return pl.pallas_call(
        functools.partial(_ffn_kernel, pre=pre, post=post, n_sub=max(1, tm // FFN_SUB_ROWS)),
        grid=(rows // tm,),
        in_specs=[
            pl.BlockSpec((tm, D_MODEL), lambda i: (i, 0)),
            pl.BlockSpec((None, 6, D_MODEL), lambda i: (layer, 0, 0), **const),
            pl.BlockSpec((None, D_MODEL, 2 * D_FF), lambda i: (layer, 0, 0), **const),
            pl.BlockSpec((None, D_FF, D_MODEL), lambda i: (layer, 0, 0), **const),
        ],
        out_specs=pl.BlockSpec((tm, D_MODEL), lambda i: (i, 0)),
        out_shape=jax.ShapeDtypeStruct(x2d.shape, F32),
        compiler_params=pltpu.CompilerParams(
            dimension_semantics=("arbitrary",), vmem_limit_bytes=VMEM_LIMIT),
        name=f"ffn_l{layer}_{pre}",
    )(x2d, gains, wgu, wdn)


def _cumsum_rows(x):
    n = x.shape[0]
    row = lax.broadcasted_iota(jnp.int32, x.shape, 0)
    s = 1
    while s < n:
        x = x + jnp.where(row >= s, pltpu.roll(x, s, axis=0), 0.0)
        s *= 2
    return x


def _block_row(x, blk, r):
    n, w = x.shape
    if blk >= SUBLANES:
        y = x.reshape(n // blk, blk, w)[:, r:r + 1, :]
        return jnp.broadcast_to(y, (n // blk, blk, w)).reshape(n, w)
    row = lax.broadcasted_iota(jnp.int32, x.shape, 0) & (SUBLANES - 1)
    if blk == 4:
        assert r == 1
        return jnp.where(row < 4, _block_row(x, SUBLANES, 1), _block_row(x, SUBLANES, 5))
    assert blk == 2 and r == 0
    return jnp.where((row & 1) == 1, pltpu.roll(x, 1, axis=0), x)


def _hgrn_chunk_stages(load, st_ref, seq, store):
    heads = [slice(hd * C_HEAD_K, (hd + 1) * C_HEAD_K) for hd in range(C_HEADS)]
    val = {}

    def masks(n):
        rr = lax.broadcasted_iota(jnp.int32, (n, n), 0)
        cc = lax.broadcasted_iota(jnp.int32, (n, n), 1)
        return rr, cc, rr ^ cc, rr > cc

    def prefix():
        q, k, logf = load('q'), load('k'), load('lf')
        n = q.shape[0]
        rr, cc, xr, lower = masks(n)
        p0 = q * k
        p1 = q * jnp.exp(logf) * pltpu.roll(k, 1, axis=0)
        scores = []
        for sl in heads:
            d0 = jnp.sum(p0[:, sl], axis=-1, keepdims=True)
            d1 = jnp.sum(p1[:, sl], axis=-1, keepdims=True)
            scores.append(jnp.where(rr == cc, d0, jnp.where((xr == 1) & lower, d1, 0.0)))
        val.update(q=q, k=k, b=_cumsum_rows(logf), scores=scores)

    def level(h):
        q, k, b = val['q'], val['k'], val['b']
        n = q.shape[0]
        _, _, xr, lower = masks(n)
        row = lax.broadcasted_iota(jnp.int32, (n, C_WIDTH), 0)
        bmid = _block_row(b, 2 * h, h - 1)
        m = jnp.where((row & h) != 0, q, k) * jnp.exp2(jnp.abs(b - bmid) * (-LOG2E))
        m = m.astype(BF16)
        pair = ((xr & -h) == h) & lower
        for hd, sl in enumerate(heads):
            val['scores'][hd] = jnp.where(pair, _dot_nt(m[:, sl], m[:, sl]), val['scores'][hd])

    def state():
        q, k, b = val['q'], val['k'], val['b']
        n = q.shape[0]
        vb = load('v').astype(BF16)
        b_last = b[n - 1:n, :]
        qb = (q * jnp.exp(b)).astype(BF16)
        kb = (k * jnp.exp(b_last - b)).astype(BF16)
        decay = jnp.exp(b_last)
        outs = []
        for hd, sl in enumerate(heads):
            st = st_ref[seq, hd]
            outs.append(_dot(val['scores'][hd].astype(BF16), vb[:, sl])
                        + _dot_nt(qb[:, sl], st.astype(BF16)))
            st_ref[seq, hd] = decay[:, sl] * st + _dot_tn(vb[:, sl], kb[:, sl])
        store(jnp.concatenate(outs, axis=1))

    return prefix, level, state


def _emit(*streams):
    order = []
    for k, stream in enumerate(streams):
        order += [((i + 0.5) / len(stream), k, i) for i in range(len(stream))]
    for _, k, i in sorted(order):
        streams[k][i]()


P_U, P_V, P_Q, P_F, P_I, P_G = (slice(i * C_WIDTH, (i + 1) * C_WIDTH) for i in range(6))


def _mixer_kernel(x_ref, pool0_ref, s0_ref, g_ref, win_ref, vgain_ref, ws_ref, bs_ref, wpool_ref,
                  pscale_ref, lbraw_ref, cgain_ref, wbr_ref, wout_ref,
                  xo_ref, poolo_ref, so_ref, *rest,
                  layer, n_seq, seq_tile, chunk, n_valid, emit_v, n_sub):
    if emit_v:
        vo_ref, rest = rest[0], rest[1:]
    p_s, k_s, ob_s, oa_s, oc_s, g_s, st_s, ext_s = rest
    rows = n_seq * seq_tile
    sub_rows = rows // n_sub
    t_idx = pl.program_id(1)

    @pl.when(t_idx == 0)
    def _init():
        for s in range(n_seq):
            for hd in range(C_HEADS):
                st_s[s, hd] = s0_ref[s, hd].T
            ext_s[s, 0:1, :] = jnp.zeros((1, B_WIDTH), F32)
            ext_s[s, POOL_PAD - POOL_BUF:POOL_PAD, :] = pool0_ref[s]

    lbr = lbraw_ref[...]
    e = jnp.exp(lbr - jnp.max(lbr, axis=0, keepdims=True))
    soft = e / jnp.sum(e, axis=0, keepdims=True)
    lb = jnp.zeros((1, C_WIDTH), F32)
    for j in range(1, layer + 1):
        lb = lb + soft[j:j + 1, :]

    tri = (lax.broadcasted_iota(jnp.int32, (chunk, chunk), 0)
           >= lax.broadcasted_iota(jnp.int32, (chunk, chunk), 1))
    w_mix = [jnp.where(tri, ws_ref[hd, 0:chunk, 0:chunk], 0.0).astype(BF16) for hd in range(A_HEADS)]

    def segments(j):
        if n_seq == 1:
            return [(0, j * sub_rows, j * sub_rows, sub_rows)]
        per = sub_rows // seq_tile
        return [(j * per + i, 0, (j * per + i) * seq_tile, seq_tile) for i in range(per)]

    def load_x(j):
        if n_seq == 1:
            return x_ref[0, j * sub_rows:(j + 1) * sub_rows, :]
        per = sub_rows // seq_tile
        return x_ref[j * per:(j + 1) * per].reshape(sub_rows, D_MODEL)

    def store_rows(ref, j, val):
        if n_seq == 1:
            ref[0, j * sub_rows:(j + 1) * sub_rows, :] = val
        else:
            per = sub_rows // seq_tile
            ref[j * per:(j + 1) * per] = val.reshape(per, seq_tile, val.shape[-1])

    class Sub:
        def __init__(self, j):
            self.j = j
            self.rs = slice(j * sub_rows, (j + 1) * sub_rows)
            self.h = None
            self.y = {}

        def norm(self):
            self.h = _rms(load_x(self.j), g_ref[2:3, :]).astype(BF16)

        def proj(self, off, width):
            return _dot(self.h, win_ref[:, off:off + width])

        def proj_thunks(self):
            rs = self.rs

            def to_p(slot, off):
                def run():
                    p_s[rs, slot] = self.proj(off, C_WIDTH)
                return run

            def pool_in():
                xb = self.proj(OFF_POOL, B_WIDTH)
                for s, off, r0, n in segments(self.j):
                    ext_s[s, POOL_PAD + off:POOL_PAD + off + n, :] = xb[r0 - rs.start:r0 - rs.start + n]

            return [to_p(P_U, OFF_UV), to_p(P_V, OFF_UV + A_WIDTH), pool_in,
                    to_p(P_Q, OFF_HGRN), to_p(P_F, OFF_HGRN + C_WIDTH),
                    to_p(P_I, OFF_HGRN + 2 * C_WIDTH), to_p(P_G, OFF_HGRN + 3 * C_WIDTH)]

        def prep_thunks(self):
            rs = self.rs

            def gelu_u():
                p_s[rs, P_U] = jax.nn.gelu(p_s[rs, P_U])

            def gelu_v():
                v = _rms(jax.nn.gelu(p_s[rs, P_V]), vgain_ref[...])
                p_s[rs, P_V] = v
                if emit_v:
                    store_rows(vo_ref, self.j, v)

            def pool():
                for s, off, r0, n in segments(self.j):
                    t_loc = lax.broadcasted_iota(jnp.int32, (n, 1), 0)
                    avail = t_idx * seq_tile + off + t_loc + (1 + n_valid)
                    parts = []
                    for g, w in enumerate(POOL_WINDOWS):
                        sl = slice(g * B_GROUP, (g + 1) * B_GROUP)
                        base = POOL_PAD + off
                        x0 = ext_s[s, base:base + n, sl]
                        acc = x0
                        for d in range(1, w):
                            acc = acc + ext_s[s, base - d:base - d + n, sl]
                        inv = 1.0 / jnp.minimum(avail, w).astype(F32)
                        pooled = acc * inv - x0
                        parts.append(_dot(pooled.astype(BF16), wpool_ref[g].astype(BF16)))
                    ob_s[r0:r0 + n, :] = jnp.concatenate(parts, axis=1) * pscale_ref[...]

            def prep_q():
                p_s[rs, P_Q] = jax.nn.silu(p_s[rs, P_Q]) * (C_HEAD_K ** -0.5)

            def prep_f():
                z = p_s[rs, P_F]
                f = lb + (1.0 - lb) * jax.nn.sigmoid(z)
                p_s[rs, P_F] = jnp.log(jnp.maximum(f, F_FLOOR))
                k_s[rs, :] = (1.0 - lb) * jax.nn.sigmoid(-z)

            return [gelu_u, gelu_v, pool, prep_q, prep_f]

        def gate_thunks(self):
            rs = self.rs

            def piece(p):
                def run():
                    cols = slice(p * C_WIDTH, (p + 1) * C_WIDTH)
                    g_s[rs, cols] = self.proj(OFF_GATES + p * C_WIDTH, C_WIDTH)
                return run

            return [piece(p) for p in range(N_BRANCH * D_MODEL // C_WIDTH)]

        def chunk_thunks(self):
            out = []
            for c in range(self.rs.start // chunk, self.rs.stop // chunk):
                cs = slice(c * chunk, (c + 1) * chunk)

                def gmlp(cs=cs):
                    vb = p_s[cs, P_V].astype(BF16)
                    for hd in range(A_HEADS):
                        sl = slice(hd * A_HEAD, (hd + 1) * A_HEAD)
                        mixed = _dot(w_mix[hd], vb[:, sl]) + bs_ref[0:chunk, hd:hd + 1]
                        oa_s[cs, sl] = p_s[cs, sl] * mixed

                def load(name, cs=cs):
                    if name == 'k':
                        return k_s[cs, :]
                    return p_s[cs, {'q': P_Q, 'lf': P_F, 'v': P_I}[name]]

                def store(o, cs=cs):
                    oc_s[cs, :] = o

                prefix, level, state = _hgrn_chunk_stages(load, st_s, (c * chunk) // seq_tile, store)
                out += [gmlp, prefix]
                h = chunk // 2
                while h >= 2:
                    out.append(functools.partial(level, h))
                    h //= 2
                out.append(state)
            return out

        def branch_thunks(self):
            rs = self.rs

            def ya():
                self.y['a'] = _dot(oa_s[rs, :].astype(BF16), wbr_ref[0:A_WIDTH, :])

            def yb():
                self.y['b'] = _dot(ob_s[rs, :].astype(BF16), wbr_ref[A_WIDTH:A_WIDTH + B_WIDTH, :])

            def yc():
                oc = oc_s[rs, :]
                cg = cgain_ref[...]
                oc = jnp.concatenate(
                    [_rms(oc[:, hd * C_HEAD_V:(hd + 1) * C_HEAD_V], cg) for hd in range(C_HEADS)], axis=1)
                o_c = oc * jax.nn.silu(p_s[rs, P_G])
                self.y['c'] = _dot(o_c.astype(BF16), wbr_ref[A_WIDTH + B_WIDTH:, :])

            return [yb, ya, yc]

        def merge_thunks(self):
            rs = self.rs

            def merge():
                m = jax.nn.sigmoid(g_s[rs, 0:D_MODEL]) * self.y['a']
                m = m + jax.nn.sigmoid(g_s[rs, D_MODEL:2 * D_MODEL]) * self.y['b']
                m = m + jax.nn.sigmoid(g_s[rs, 2 * D_MODEL:]) * self.y['c']
                self.y['m'] = m.astype(BF16)

            def out():
                self.y['o'] = _dot(self.y['m'], wout_ref[...])

            return [merge, out]

        def finish(self):
            store_rows(xo_ref, self.j, load_x(self.j) + _rms(self.y['o'], g_ref[3:4, :]))

    subs = [Sub(j) for j in range(n_sub)]
    if n_sub == 1:
        a, = subs
        a.norm()
        _emit(a.proj_thunks())
        _emit(a.prep_thunks())
        _emit(a.gate_thunks(), a.chunk_thunks())
        _emit(a.branch_thunks() + a.merge_thunks())
        a.finish()
    else:
        a, b = subs
        a.norm()
        _emit(a.proj_thunks())
        b.norm()
        _emit(b.proj_thunks(), a.prep_thunks())
        _emit(a.gate_thunks() + b.gate_thunks(), a.chunk_thunks() + b.prep_thunks())
        _emit(a.branch_thunks() + a.merge_thunks(), b.chunk_thunks())
        a.finish()
        _emit(b.branch_thunks() + b.merge_thunks())
        b.finish()

    for s in range(n_seq):
        poolo_ref[s] = ext_s[s, seq_tile + POOL_PAD - POOL_BUF:seq_tile + POOL_PAD, :]
        ext_s[s, 0:POOL_PAD, :] = ext_s[s, seq_tile:seq_tile + POOL_PAD, :]

    @pl.when(t_idx == pl.num_programs(1) - 1)
    def _final():
        for s in range(n_seq):
            for hd in range(C_HEADS):
                so_ref[s, hd] = st_s[s, hd].T


def _mixer(x, pool0, s0, gains, w_in, vgain, w_s, b_sT, w_pool, pscale, lb_raw, cgain, w_br, w_out,
           layer, n_seq, seq_tile, n_valid, emit_v):
    bsz, t, _ = x.shape
    chunk = min(seq_tile, GMLP_CHUNK)
    rows = n_seq * seq_tile
    grid = (bsz // n_seq, t // seq_tile)
    const = dict(pipeline_mode=pl.Buffered(1))

    def lsel(*tail):
        return lambda b, i: (layer,) + tail

    in_specs = [
        pl.BlockSpec((n_seq, seq_tile, D_MODEL), lambda b, i: (b, i, 0)),
        pl.BlockSpec((n_seq, POOL_BUF, B_WIDTH), lambda b, i: (b, 0, 0)),
        pl.BlockSpec((n_seq, C_HEADS, C_HEAD_K, C_HEAD_V), lambda b, i: (b, 0, 0, 0)),
        pl.BlockSpec((None, 6, D_MODEL), lsel(0, 0), **const),
        pl.BlockSpec((None, D_MODEL, IN_COLS), lsel(0, 0), **const),
        pl.BlockSpec((None, 1, A_WIDTH), lsel(0, 0), **const),
        pl.BlockSpec((None, A_HEADS, GMLP_CHUNK, GMLP_CHUNK), lsel(0, 0, 0), **const),
        pl.BlockSpec((None, GMLP_CHUNK, A_HEADS), lsel(0, 0), **const),
        pl.BlockSpec((None, B_GROUPS, B_GROUP, B_GROUP), lsel(0, 0, 0), **const),
        pl.BlockSpec((None, 1, B_WIDTH), lsel(0, 0), **const),
        pl.BlockSpec((DEPTH, C_WIDTH), lambda b, i: (0, 0), **const),
        pl.BlockSpec((None, 1, C_HEAD_V), lsel(0, 0), **const),
        pl.BlockSpec((None, N_BRANCH * A_WIDTH, D_MODEL), lsel(0, 0), **const),
        pl.BlockSpec((None, D_MODEL, D_MODEL), lsel(0, 0), **const),
    ]
    out_specs = [
        pl.BlockSpec((n_seq, seq_tile, D_MODEL), lambda b, i: (b, i, 0)),
        pl.BlockSpec((n_seq, POOL_BUF, B_WIDTH), lambda b, i: (b, 0, 0)),
        pl.BlockSpec((n_seq, C_HEADS, C_HEAD_K, C_HEAD_V), lambda b, i: (b, 0, 0, 0)),
    ]
    out_shape = [
        jax.ShapeDtypeStruct(x.shape, F32),
        jax.ShapeDtypeStruct((bsz, POOL_BUF, B_WIDTH), F32),
        jax.ShapeDtypeStruct((bsz, C_HEADS, C_HEAD_K, C_HEAD_V), F32),
    ]
    if emit_v:
        out_specs.append(pl.BlockSpec((n_seq, seq_tile, A_WIDTH), lambda b, i: (b, i, 0)))
        out_shape.append(jax.ShapeDtypeStruct((bsz, t, A_WIDTH), F32))
    act = pltpu.VMEM((rows, C_WIDTH), F32)
    scratch = [pltpu.VMEM((rows, 6 * C_WIDTH), F32)] + [act] * 4 + [
        pltpu.VMEM((rows, N_BRANCH * D_MODEL), F32),
        pltpu.VMEM((n_seq, C_HEADS, C_HEAD_V, C_HEAD_K), F32),
        pltpu.VMEM((n_seq, POOL_PAD + seq_tile, B_WIDTH), F32),
    ]
    return pl.pallas_call(
        functools.partial(_mixer_kernel, layer=layer, n_seq=n_seq, seq_tile=seq_tile, chunk=chunk,
                          n_valid=n_valid, emit_v=emit_v, n_sub=min(2, max(1, rows // MIX_SUB_ROWS))),
        grid=grid,
        in_specs=in_specs,
        out_specs=out_specs,
        out_shape=out_shape,
        scratch_shapes=scratch,
        compiler_params=pltpu.CompilerParams(
            dimension_semantics=("arbitrary", "arbitrary"), vmem_limit_bytes=VMEM_LIMIT),
        name=f"mixer_l{layer}_{'dec' if emit_v else 'pre'}",
    )(x, pool0, s0, gains, w_in, vgain, w_s, b_sT, w_pool, pscale, lb_raw, cgain, w_br, w_out)


def kernel(x_prompt, x_sample, state_pool, state_hgrn, norm_gains, w_ffn1_gu, w_ffn1_down, w_in,
           gmlp_v_gain, gmlp_w_s, gmlp_b_s, pool_w, pool_scale, hgrn_lower_bounds, hgrn_out_gain,
           w_branch, w_out, w_ffn2_gu, w_ffn2_down):
    bp, tp, _ = x_prompt.shape
    bs, ts, _ = x_sample.shape
    w1gu, w1dn = w_ffn1_gu.astype(BF16), w_ffn1_down.astype(BF16)
    w2gu, w2dn = w_ffn2_gu.astype(BF16), w_ffn2_down.astype(BF16)
    win, wbr, wout = w_in.astype(BF16), w_branch.astype(BF16), w_out.astype(BF16)
    vgain = gmlp_v_gain.reshape(DEPTH, 1, A_WIDTH)
    pscale = pool_scale.reshape(DEPTH, 1, B_WIDTH)
    cgain = hgrn_out_gain.reshape(DEPTH, 1, C_HEAD_V)
    b_sT = jnp.swapaxes(gmlp_b_s, 1, 2)
    zero_pool = jnp.zeros((bp, POOL_BUF, B_WIDTH), F32)
    zero_state = jnp.zeros((bp, C_HEADS, C_HEAD_K, C_HEAD_V), F32)

    xp, xs = x_prompt, x_sample
    pool_p, pool_s, hg_p, hg_s, v_s = [], [], [], [], []
    for l in range(DEPTH):
        shared = (norm_gains, win, vgain, gmlp_w_s, b_sT, pool_w, pscale, hgrn_lower_bounds, cgain,
                  wbr, wout)
        xp = _ffn(xp.reshape(bp * tp, D_MODEL), norm_gains, w1gu, w1dn, l, 0, 1).reshape(bp, tp, D_MODEL)
        xs = _ffn(xs.reshape(bs * ts, D_MODEL), norm_gains, w1gu, w1dn, l, 0, 1).reshape(bs, ts, D_MODEL)
        xp, pp, sp = _mixer(xp, zero_pool, zero_state, *shared, layer=l, n_seq=1,
                            seq_tile=min(MIX_ROWS, tp), n_valid=0, emit_v=False)
        xs, ps, ss, vs = _mixer(xs, state_pool[l], state_hgrn[l], *shared, layer=l, n_seq=bs,
                                seq_tile=ts, n_valid=POOL_BUF, emit_v=True)
        xp = _ffn(xp.reshape(bp * tp, D_MODEL), norm_gains, w2gu, w2dn, l, 4, 5).reshape(bp, tp, D_MODEL)
        xs = _ffn(xs.reshape(bs * ts, D_MODEL), norm_gains, w2gu, w2dn, l, 4, 5).reshape(bs, ts, D_MODEL)
        pool_p.append(pp)
        pool_s.append(ps)
        hg_p.append(sp)
        hg_s.append(ss)
        v_s.append(vs)
    return (xp, xs, jnp.stack(pool_p), jnp.stack(pool_s), jnp.stack(hg_p), jnp.stack(hg_s),
            jnp.stack(v_s))
```

```python
import functools

import jax
import jax.numpy as jnp
from jax import lax
from jax.experimental import pallas as pl
from jax.experimental.pallas import tpu as pltpu

D_MODEL = 1024
DEPTH = 4
D_FF = 2816
A_HEADS = 4
A_HEAD = 128
A_WIDTH = A_HEADS * A_HEAD
GMLP_CHUNK = 128
B_GROUPS = 4
B_GROUP = 128
B_WIDTH = B_GROUPS * B_GROUP
POOL_WINDOWS = (2, 4, 8, 16)
POOL_BUF = 15
POOL_PAD = 16
C_HEADS = 4
C_HEAD_K = 128
C_HEAD_V = 128
C_WIDTH = C_HEADS * C_HEAD_V
N_BRANCH = 3
EPS = 1e-6
F_FLOOR = 1e-30
LOG2E = 1.4426950408889634

OFF_UV = 0
OFF_POOL = 2 * A_WIDTH
OFF_HGRN = OFF_POOL + B_WIDTH
OFF_GATES = OFF_HGRN + 4 * C_WIDTH
IN_COLS = OFF_GATES + N_BRANCH * D_MODEL

SUBLANES = 8
FFN_ROWS = 1024
FFN_SUB_ROWS = 256
MIX_ROWS = 512
MIX_SUB_ROWS = 256
VMEM_LIMIT = 56 * 1024 * 1024

F32 = jnp.float32
BF16 = jnp.bfloat16


def _rms(x, g):
    ms = jnp.mean(x * x, axis=-1, keepdims=True)
    return x * lax.rsqrt(ms + EPS) * g


def _dot(a, b):
    return jnp.dot(a, b, preferred_element_type=F32)


def _dot_nt(a, b):
    return lax.dot_general(a, b, (((1,), (1,)), ((), ())), preferred_element_type=F32)


def _dot_tn(a, b):
    return lax.dot_general(a, b, (((0,), (0,)), ((), ())), preferred_element_type=F32)


def _ffn_kernel(x_ref, g_ref, wgu_ref, wdn_ref, o_ref, *, pre, post, n_sub):
    sub = x_ref.shape[0] // n_sub
    for s in range(n_sub):
        rs = slice(s * sub, (s + 1) * sub)
        x = x_ref[rs, :]
        h = _rms(x, g_ref[pre:pre + 1, :]).astype(BF16)
        g = _dot(h, wgu_ref[:, :D_FF])
        u = _dot(h, wgu_ref[:, D_FF:])
        a = (jax.nn.silu(g) * u).astype(BF16)
        y = _dot(a, wdn_ref[...])
        o_ref[rs, :] = x + 0.5 * _rms(y, g_ref[post:post + 1, :])


def _ffn(x2d, gains, wgu, wdn, layer, pre, post):
    rows = x2d.shape[0]
    tm = min(FFN_ROWS, rows)
    const = dict(pipeline_mode=pl.Buffered(1))
    return pl.pallas_call(
        functools.partial(_ffn_kernel, pre=pre, post=post, n_sub=max(1, tm // FFN_SUB_ROWS)),
        grid=(rows // tm,),
        in_specs=[
            pl.BlockSpec((tm, D_MODEL), lambda i: (i, 0)),
            pl.BlockSpec((None, 6, D_MODEL), lambda i: (layer, 0, 0), **const),
            pl.BlockSpec((None, D_MODEL, 2 * D_FF), lambda i: (layer, 0, 0), **const),
            pl.BlockSpec((None, D_FF, D_MODEL), lambda i: (layer, 0, 0), **const),
        ],
        out_specs=pl.BlockSpec((tm, D_MODEL), lambda i: (i, 0)),
        out_shape=jax.ShapeDtypeStruct(x2d.shape, F32),
        compiler_params=pltpu.CompilerParams(
            dimension_semantics=("arbitrary",), vmem_limit_bytes=VMEM_LIMIT),
        name=f"ffn_l{layer}_{pre}",
    )(x2d, gains, wgu, wdn)


def _cumsum_rows(x):
    n = x.shape[0]
    row = lax.broadcasted_iota(jnp.int32, x.shape, 0)
    s = 1
    while s < n:
        x = x + jnp.where(row >= s, pltpu.roll(x, s, axis=0), 0.0)
        s *= 2
    return x


def _block_row(x, blk, r):
    n, w = x.shape
    if blk >= SUBLANES:
        y = x.reshape(n // blk, blk, w)[:, r:r + 1, :]
        return jnp.broadcast_to(y, (n // blk, blk, w)).reshape(n, w)
    row = lax.broadcasted_iota(jnp.int32, x.shape, 0) & (SUBLANES - 1)
    if blk == 4:
        assert r == 1
        return jnp.where(row < 4, _block_row(x, SUBLANES, 1), _block_row(x, SUBLANES, 5))
    assert blk == 2 and r == 0
    return jnp.where((row & 1) == 1, pltpu.roll(x, 1, axis=0), x)


def _hgrn_chunk_stages(load, st_ref, seq, store):
    heads = [slice(hd * C_HEAD_K, (hd + 1) * C_HEAD_K) for hd in range(C_HEADS)]
    val = {}

    def masks(n):
        rr = lax.broadcasted_iota(jnp.int32, (n, n), 0)
        cc = lax.broadcasted_iota(jnp.int32, (n, n), 1)
        return rr, cc, rr ^ cc, rr > cc

    def prefix():
        q, k, logf = load('q'), load('k'), load('lf')
        n = q.shape[0]
        rr, cc, xr, lower = masks(n)
        p0 = q * k
        p1 = q * jnp.exp(logf) * pltpu.roll(k, 1, axis=0)
        scores = []
        for sl in heads:
            d0 = jnp.sum(p0[:, sl], axis=-1, keepdims=True)
            d1 = jnp.sum(p1[:, sl], axis=-1, keepdims=True)
            scores.append(jnp.where(rr == cc, d0, jnp.where((xr == 1) & lower, d1, 0.0)))
        val.update(q=q, k=k, b=_cumsum_rows(logf), scores=scores)

    def level(h):
        q, k, b = val['q'], val['k'], val['b']
        n = q.shape[0]
        _, _, xr, lower = masks(n)
        row = lax.broadcasted_iota(jnp.int32, (n, C_WIDTH), 0)
        bmid = _block_row(b, 2 * h, h - 1)
        m = jnp.where((row & h) != 0, q, k) * jnp.exp2(jnp.abs(b - bmid) * (-LOG2E))
        m = m.astype(BF16)
        pair = ((xr & -h) == h) & lower
        for hd, sl in enumerate(heads):
            val['scores'][hd] = jnp.where(pair, _dot_nt(m[:, sl], m[:, sl]), val['scores'][hd])

    def state():
        q, k, b = val['q'], val['k'], val['b']
        n = q.shape[0]
        vb = load('v').astype(BF16)
        b_last = b[n - 1:n, :]
        qb = (q * jnp.exp(b)).astype(BF16)
        kb = (k * jnp.exp(b_last - b)).astype(BF16)
        decay = jnp.exp(b_last)
        outs = []
        for hd, sl in enumerate(heads):
            st = st_ref[seq, hd]
            outs.append(_dot(val['scores'][hd].astype(BF16), vb[:, sl])
                        + _dot_nt(qb[:, sl], st.astype(BF16)))
            st_ref[seq, hd] = decay[:, sl] * st + _dot_tn(vb[:, sl], kb[:, sl])
        store(jnp.concatenate(outs, axis=1))

    return prefix, level, state


def _emit(*streams):
    order = []
    for k, stream in enumerate(streams):
        order += [((i + 0.5) / len(stream), k, i) for i in range(len(stream))]
    for _, k, i in sorted(order):
        streams[k][i]()


P_U, P_V, P_Q, P_F, P_I, P_G = (slice(i * C_WIDTH, (i + 1) * C_WIDTH) for i in range(6))


def _mixer_kernel(x_ref, pool0_ref, s0_ref, g_ref, win_ref, vgain_ref, ws_ref, bs_ref, wpool_ref,
                  pscale_ref, lbraw_ref, cgain_ref, wbr_ref, wout_ref,
                  xo_ref, poolo_ref, so_ref, *rest,
                  layer, n_seq, seq_tile, chunk, n_valid, emit_v, n_sub):
    if emit_v:
        vo_ref, rest = rest[0], rest[1:]
    p_s, k_s, ob_s, oa_s, oc_s, g_s, st_s, ext_s = rest
    rows = n_seq * seq_tile
    sub_rows = rows // n_sub
    t_idx = pl.program_id(1)

    @pl.when(t_idx == 0)
    def _init():
        for s in range(n_seq):
            for hd in range(C_HEADS):
                st_s[s, hd] = s0_ref[s, hd].T
            ext_s[s, 0:1, :] = jnp.zeros((1, B_WIDTH), F32)
            ext_s[s, POOL_PAD - POOL_BUF:POOL_PAD, :] = pool0_ref[s]

    lbr = lbraw_ref[...]
    e = jnp.exp(lbr - jnp.max(lbr, axis=0, keepdims=True))
    soft = e / jnp.sum(e, axis=0, keepdims=True)
    lb = jnp.zeros((1, C_WIDTH), F32)
    for j in range(1, layer + 1):
        lb = lb + soft[j:j + 1, :]

    tri = (lax.broadcasted_iota(jnp.int32, (chunk, chunk), 0)
           >= lax.broadcasted_iota(jnp.int32, (chunk, chunk), 1))
    w_mix = [jnp.where(tri, ws_ref[hd, 0:chunk, 0:chunk], 0.0).astype(BF16) for hd in range(A_HEADS)]

    def segments(j):
        if n_seq == 1:
            return [(0, j * sub_rows, j * sub_rows, sub_rows)]
        per = sub_rows // seq_tile
        return [(j * per + i, 0, (j * per + i) * seq_tile, seq_tile) for i in range(per)]

    def load_x(j):
        if n_seq == 1:
            return x_ref[0, j * sub_rows:(j + 1) * sub_rows, :]
        per = sub_rows // seq_tile
        return x_ref[j * per:(j + 1) * per].reshape(sub_rows, D_MODEL)

    def store_rows(ref, j, val):
        if n_seq == 1:
            ref[0, j * sub_rows:(j + 1) * sub_rows, :] = val
        else:
            per = sub_rows // seq_tile
            ref[j * per:(j + 1) * per] = val.reshape(per, seq_tile, val.shape[-1])

    class Sub:
        def __init__(self, j):
            self.j = j
            self.rs = slice(j * sub_rows, (j + 1) * sub_rows)
            self.h = None
            self.y = {}

        def norm(self):
            self.h = _rms(load_x(self.j), g_ref[2:3, :]).astype(BF16)

        def proj(self, off, width):
            return _dot(self.h, win_ref[:, off:off + width])

        def proj_thunks(self):
            rs = self.rs

            def to_p(slot, off):
                def run():
                    p_s[rs, slot] = self.proj(off, C_WIDTH)
                return run

            def pool_in():
                xb = self.proj(OFF_POOL, B_WIDTH)
                for s, off, r0, n in segments(self.j):
                    ext_s[s, POOL_PAD + off:POOL_PAD + off + n, :] = xb[r0 - rs.start:r0 - rs.start + n]

            return [to_p(P_U, OFF_UV), to_p(P_V, OFF_UV + A_WIDTH), pool_in,
                    to_p(P_Q, OFF_HGRN), to_p(P_F, OFF_HGRN + C_WIDTH),
                    to_p(P_I, OFF_HGRN + 2 * C_WIDTH), to_p(P_G, OFF_HGRN + 3 * C_WIDTH)]

        def prep_thunks(self):
            rs = self.rs

            def gelu_u():
                p_s[rs, P_U] = jax.nn.gelu(p_s[rs, P_U])

            def gelu_v():
                v = _rms(jax.nn.gelu(p_s[rs, P_V]), vgain_ref[...])
                p_s[rs, P_V] = v
                if emit_v:
                    store_rows(vo_ref, self.j, v)

            def pool():
                for s, off, r0, n in segments(self.j):
                    t_loc = lax.broadcasted_iota(jnp.int32, (n, 1), 0)
                    avail = t_idx * seq_tile + off + t_loc + (1 + n_valid)
                    parts = []
                    for g, w in enumerate(POOL_WINDOWS):
                        sl = slice(g * B_GROUP, (g + 1) * B_GROUP)
                        win = ext_s[s, off:off + POOL_PAD + n, sl]
                        x0 = win[POOL_PAD:]
                        d = 1
                        while d < w:
                            win = win + pltpu.roll(win, d, axis=0)
                            d *= 2
                        inv = 1.0 / jnp.minimum(avail, w).astype(F32)
                        pooled = win[POOL_PAD:] * inv - x0
                        parts.append(_dot(pooled.astype(BF16), wpool_ref[g].astype(BF16)))
                    ob_s[r0:r0 + n, :] = jnp.concatenate(parts, axis=1) * pscale_ref[...]

            def prep_q():
                p_s[rs, P_Q] = jax.nn.silu(p_s[rs, P_Q]) * (C_HEAD_K ** -0.5)

            def prep_f():
                z = p_s[rs, P_F]
                f = lb + (1.0 - lb) * jax.nn.sigmoid(z)
                p_s[rs, P_F] = jnp.log(jnp.maximum(f, F_FLOOR))
                k_s[rs, :] = (1.0 - lb) * jax.nn.sigmoid(-z)

            return [gelu_u, gelu_v, pool, prep_q, prep_f]

        def gate_thunks(self):
            rs = self.rs

            def piece(p):
                def run():
                    cols = slice(p * C_WIDTH, (p + 1) * C_WIDTH)
                    g_s[rs, cols] = self.proj(OFF_GATES + p * C_WIDTH, C_WIDTH)
                return run

            return [piece(p) for p in range(N_BRANCH * D_MODEL // C_WIDTH)]

        def chunk_thunks(self):
            out = []
            for c in range(self.rs.start // chunk, self.rs.stop // chunk):
                cs = slice(c * chunk, (c + 1) * chunk)

                def gmlp(cs=cs):
                    vb = p_s[cs, P_V].astype(BF16)
                    for hd in range(A_HEADS):
                        sl = slice(hd * A_HEAD, (hd + 1) * A_HEAD)
                        mixed = _dot(w_mix[hd], vb[:, sl]) + bs_ref[0:chunk, hd:hd + 1]
                        oa_s[cs, sl] = p_s[cs, sl] * mixed

                def load(name, cs=cs):
                    if name == 'k':
                        return k_s[cs, :]
                    return p_s[cs, {'q': P_Q, 'lf': P_F, 'v': P_I}[name]]

                def store(o, cs=cs):
                    oc_s[cs, :] = o

                prefix, level, state = _hgrn_chunk_stages(load, st_s, (c * chunk) // seq_tile, store)
                out += [gmlp, prefix]
                h = chunk // 2
                while h >= 2:
                    out.append(functools.partial(level, h))
                    h //= 2
                out.append(state)
            return out

        def branch_thunks(self):
            rs = self.rs

            def ya():
                self.y['a'] = _dot(oa_s[rs, :].astype(BF16), wbr_ref[0:A_WIDTH, :])

            def yb():
                self.y['b'] = _dot(ob_s[rs, :].astype(BF16), wbr_ref[A_WIDTH:A_WIDTH + B_WIDTH, :])

            def yc():
                oc = oc_s[rs, :]
                cg = cgain_ref[...]
                oc = jnp.concatenate(
                    [_rms(oc[:, hd * C_HEAD_V:(hd + 1) * C_HEAD_V], cg) for hd in range(C_HEADS)], axis=1)
                o_c = oc * jax.nn.silu(p_s[rs, P_G])
                self.y['c'] = _dot(o_c.astype(BF16), wbr_ref[A_WIDTH + B_WIDTH:, :])

            return [yb, ya, yc]

        def merge_thunks(self):
            rs = self.rs

            def merge():
                m = jax.nn.sigmoid(g_s[rs, 0:D_MODEL]) * self.y['a']
                m = m + jax.nn.sigmoid(g_s[rs, D_MODEL:2 * D_MODEL]) * self.y['b']
                m = m + jax.nn.sigmoid(g_s[rs, 2 * D_MODEL:]) * self.y['c']
                self.y['m'] = m.astype(BF16)

            def out():
                self.y['o'] = _dot(self.y['m'], wout_ref[...])

            return [merge, out]

        def finish(self):
            store_rows(xo_ref, self.j, load_x(self.j) + _rms(self.y['o'], g_ref[3:4, :]))

    subs = [Sub(j) for j in range(n_sub)]
    if n_sub == 1:
        a, = subs
        a.norm()
        _emit(a.proj_thunks())
        _emit(a.prep_thunks())
        _emit(a.gate_thunks(), a.chunk_thunks())
        _emit(a.branch_thunks() + a.merge_thunks())
        a.finish()
    else:
        a, b = subs
        a.norm()
        _emit(a.proj_thunks())
        b.norm()
        _emit(b.proj_thunks(), a.prep_thunks())
        _emit(a.gate_thunks() + b.gate_thunks(), a.chunk_thunks() + b.prep_thunks())
        _emit(a.branch_thunks() + a.merge_thunks(), b.chunk_thunks())
        a.finish()
        _emit(b.branch_thunks() + b.merge_thunks())
        b.finish()

    for s in range(n_seq):
        poolo_ref[s] = ext_s[s, seq_tile + POOL_PAD - POOL_BUF:seq_tile + POOL_PAD, :]
        ext_s[s, 0:POOL_PAD, :] = ext_s[s, seq_tile:seq_tile + POOL_PAD, :]

    @pl.when(t_idx == pl.num_programs(1) - 1)
    def _final():
        for s in range(n_seq):
            for hd in range(C_HEADS):
                so_ref[s, hd] = st_s[s, hd].T


def _mixer(x, pool0, s0, gains, w_in, vgain, w_s, b_sT, w_pool, pscale, lb_raw, cgain, w_br, w_out,
           layer, n_seq, seq_tile, n_valid, emit_v):
    bsz, t, _ = x.shape
    chunk = min(seq_tile, GMLP_CHUNK)
    rows = n_seq * seq_tile
    grid = (bsz // n_seq, t // seq_tile)
    const = dict(pipeline_mode=pl.Buffered(1))

    def lsel(*tail):
        return lambda b, i: (layer,) + tail

    in_specs = [
        pl.BlockSpec((n_seq, seq_tile, D_MODEL), lambda b, i: (b, i, 0)),
        pl.BlockSpec((n_seq, POOL_BUF, B_WIDTH), lambda b, i: (b, 0, 0)),
        pl.BlockSpec((n_seq, C_HEADS, C_HEAD_K, C_HEAD_V), lambda b, i: (b, 0, 0, 0)),
        pl.BlockSpec((None, 6, D_MODEL), lsel(0, 0), **const),
        pl.BlockSpec((None, D_MODEL, IN_COLS), lsel(0, 0), **const),
        pl.BlockSpec((None, 1, A_WIDTH), lsel(0, 0), **const),
        pl.BlockSpec((None, A_HEADS, GMLP_CHUNK, GMLP_CHUNK), lsel(0, 0, 0), **const),
        pl.BlockSpec((None, GMLP_CHUNK, A_HEADS), lsel(0, 0), **const),
        pl.BlockSpec((None, B_GROUPS, B_GROUP, B_GROUP), lsel(0, 0, 0), **const),
        pl.BlockSpec((None, 1, B_WIDTH), lsel(0, 0), **const),
        pl.BlockSpec((DEPTH, C_WIDTH), lambda b, i: (0, 0), **const),
        pl.BlockSpec((None, 1, C_HEAD_V), lsel(0, 0), **const),
        pl.BlockSpec((None, N_BRANCH * A_WIDTH, D_MODEL), lsel(0, 0), **const),
        pl.BlockSpec((None, D_MODEL, D_MODEL), lsel(0, 0), **const),
    ]
    out_specs = [
        pl.BlockSpec((n_seq, seq_tile, D_MODEL), lambda b, i: (b, i, 0)),
        pl.BlockSpec((n_seq, POOL_BUF, B_WIDTH), lambda b, i: (b, 0, 0)),
        pl.BlockSpec((n_seq, C_HEADS, C_HEAD_K, C_HEAD_V), lambda b, i: (b, 0, 0, 0)),
    ]
    out_shape = [
        jax.ShapeDtypeStruct(x.shape, F32),
        jax.ShapeDtypeStruct((bsz, POOL_BUF, B_WIDTH), F32),
        jax.ShapeDtypeStruct((bsz, C_HEADS, C_HEAD_K, C_HEAD_V), F32),
    ]
    if emit_v:
        out_specs.append(pl.BlockSpec((n_seq, seq_tile, A_WIDTH), lambda b, i: (b, i, 0)))
        out_shape.append(jax.ShapeDtypeStruct((bsz, t, A_WIDTH), F32))
    act = pltpu.VMEM((rows, C_WIDTH), F32)
    scratch = [pltpu.VMEM((rows, 6 * C_WIDTH), F32)] + [act] * 4 + [
        pltpu.VMEM((rows, N_BRANCH * D_MODEL), F32),
        pltpu.VMEM((n_seq, C_HEADS, C_HEAD_V, C_HEAD_K), F32),
        pltpu.VMEM((n_seq, POOL_PAD + seq_tile, B_WIDTH), F32),
    ]
    return pl.pallas_call(
        functools.partial(_mixer_kernel, layer=layer, n_seq=n_seq, seq_tile=seq_tile, chunk=chunk,
                          n_valid=n_valid, emit_v=emit_v, n_sub=min(2, max(1, rows // MIX_SUB_ROWS))),
        grid=grid,
        in_specs=in_specs,
        out_specs=out_specs,
        out_shape=out_shape,
        scratch_shapes=scratch,
        compiler_params=pltpu.CompilerParams(
            dimension_semantics=("arbitrary", "arbitrary"), vmem_limit_bytes=VMEM_LIMIT),
        name=f"mixer_l{layer}_{'dec' if emit_v else 'pre'}",
    )(x, pool0, s0, gains, w_in, vgain, w_s, b_sT, w_pool, pscale, lb_raw, cgain, w_br, w_out)


def kernel(x_prompt, x_sample, state_pool, state_hgrn, norm_gains, w_ffn1_gu, w_ffn1_down, w_in,
           gmlp_v_gain, gmlp_w_s, gmlp_b_s, pool_w, pool_scale, hgrn_lower_bounds, hgrn_out_gain,
           w_branch, w_out, w_ffn2_gu, w_ffn2_down):
    bp, tp, _ = x_prompt.shape
    bs, ts, _ = x_sample.shape
    w1gu, w1dn = w_ffn1_gu.astype(BF16), w_ffn1_down.astype(BF16)
    w2gu, w2dn = w_ffn2_gu.astype(BF16), w_ffn2_down.astype(BF16)
    win, wbr, wout = w_in.astype(BF16), w_branch.astype(BF16), w_out.astype(BF16)
    vgain = gmlp_v_gain.reshape(DEPTH, 1, A_WIDTH)
    pscale = pool_scale.reshape(DEPTH, 1, B_WIDTH)
    cgain = hgrn_out_gain.reshape(DEPTH, 1, C_HEAD_V)
    b_sT = jnp.swapaxes(gmlp_b_s, 1, 2)
    zero_pool = jnp.zeros((bp, POOL_BUF, B_WIDTH), F32)
    zero_state = jnp.zeros((bp, C_HEADS, C_HEAD_K, C_HEAD_V), F32)

    xp, xs = x_prompt, x_sample
    pool_p, pool_s, hg_p, hg_s, v_s = [], [], [], [], []
    for l in range(DEPTH):
        shared = (norm_gains, win, vgain, gmlp_w_s, b_sT, pool_w, pscale, hgrn_lower_bounds, cgain,
                  wbr, wout)
        xp = _ffn(xp.reshape(bp * tp, D_MODEL), norm_gains, w1gu, w1dn, l, 0, 1).reshape(bp, tp, D_MODEL)
        xs = _ffn(xs.reshape(bs * ts, D_MODEL), norm_gains, w1gu, w1dn, l, 0, 1).reshape(bs, ts, D_MODEL)
        xp, pp, sp = _mixer(xp, zero_pool, zero_state, *shared, layer=l, n_seq=1,
                            seq_tile=min(MIX_ROWS, tp), n_valid=0, emit_v=False)
        xs, ps, ss, vs = _mixer(xs, state_pool[l], state_hgrn[l], *shared, layer=l, n_seq=bs,
                                seq_tile=ts, n_valid=POOL_BUF, emit_v=True)
        xp = _ffn(xp.reshape(bp * tp, D_MODEL), norm_gains, w2gu, w2dn, l, 4, 5).reshape(bp, tp, D_MODEL)
        xs = _ffn(xs.reshape(bs * ts, D_MODEL), norm_gains, w2gu, w2dn, l, 4, 5).reshape(bs, ts, D_MODEL)
        pool_p.append(pp)
        pool_s.append(ps)
        hg_p.append(sp)
        hg_s.append(ss)
        v_s.append(vs)
    return (xp, xs, jnp.stack(pool_p), jnp.stack(pool_s), jnp.stack(hg_p), jnp.stack(hg_s),
            jnp.stack(v_s))
```

```python
import functools

import jax
import jax.numpy as jnp
from jax import lax
from jax.experimental import pallas as pl
from jax.experimental.pallas import tpu as pltpu

D_MODEL = 1024
DEPTH = 4
D_FF = 2816
A_HEADS = 4
A_HEAD = 128
A_WIDTH = A_HEADS * A_HEAD
GMLP_CHUNK = 128
B_GROUPS = 4
B_GROUP = 128
B_WIDTH = B_GROUPS * B_GROUP
POOL_WINDOWS = (2, 4, 8, 16)
POOL_BUF = 15
POOL_PAD = 16
C_HEADS = 4
C_HEAD_K = 128
C_HEAD_V = 128
C_WIDTH = C_HEADS * C_HEAD_V
N_BRANCH = 3
EPS = 1e-6
F_FLOOR = 1e-30
LOG2E = 1.4426950408889634

OFF_UV = 0
OFF_POOL = 2 * A_WIDTH
OFF_HGRN = OFF_POOL + B_WIDTH
OFF_GATES = OFF_HGRN + 4 * C_WIDTH
IN_COLS = OFF_GATES + N_BRANCH * D_MODEL

SUBLANES = 8
FFN_ROWS = 1024
FFN_SUB_ROWS = 256
MIX_ROWS = 512
MIX_SUB_ROWS = 512
VMEM_LIMIT = 56 * 1024 * 1024

F32 = jnp.float32
BF16 = jnp.bfloat16


def _rms(x, g):
    ms = jnp.mean(x * x, axis=-1, keepdims=True)
    return x * lax.rsqrt(ms + EPS) * g


def _dot(a, b):
    return jnp.dot(a, b, preferred_element_type=F32)


def _dot_nt(a, b):
    return lax.dot_general(a, b, (((1,), (1,)), ((), ())), preferred_element_type=F32)


def _dot_tn(a, b):
    return lax.dot_general(a, b, (((0,), (0,)), ((), ())), preferred_element_type=F32)


def _ffn_kernel(x_ref, g_ref, wgu_ref, wdn_ref, o_ref, *, pre, post, n_sub):
    sub = x_ref.shape[0] // n_sub
    for s in range(n_sub):
        rs = slice(s * sub, (s + 1) * sub)
        x = x_ref[rs, :]
        h = _rms(x, g_ref[pre:pre + 1, :]).astype(BF16)
        g = _dot(h, wgu_ref[:, :D_FF])
        u = _dot(h, wgu_ref[:, D_FF:])
        a = (jax.nn.silu(g) * u).astype(BF16)
        y = _dot(a, wdn_ref[...])
        o_ref[rs, :] = x + 0.5 * _rms(y, g_ref[post:post + 1, :])


def _ffn(x2d, gains, wgu, wdn, layer, pre, post):
    rows = x2d.shape[0]
    tm = min(FFN_ROWS, rows)
    const = dict(pipeline_mode=pl.Buffered(1))
    return pl.pallas_call(
        functools.partial(_ffn_kernel, pre=pre, post=post, n_sub=max(1, tm // FFN_SUB_ROWS)),
        grid=(rows // tm,),
        in_specs=[
            pl.BlockSpec((tm, D_MODEL), lambda i: (i, 0)),
            pl.BlockSpec((None, 6, D_MODEL), lambda i: (layer, 0, 0), **const),
            pl.BlockSpec((None, D_MODEL, 2 * D_FF), lambda i: (layer, 0, 0), **const),
            pl.BlockSpec((None, D_FF, D_MODEL), lambda i: (layer, 0, 0), **const),
        ],
        out_specs=pl.BlockSpec((tm, D_MODEL), lambda i: (i, 0)),
        out_shape=jax.ShapeDtypeStruct(x2d.shape, F32),
        compiler_params=pltpu.CompilerParams(
            dimension_semantics=("arbitrary",), vmem_limit_bytes=VMEM_LIMIT),
        name=f"ffn_l{layer}_{pre}",
    )(x2d, gains, wgu, wdn)


def _cumsum_rows(x):
    n = x.shape[0]
    row = lax.broadcasted_iota(jnp.int32, x.shape, 0)
    s = 1
    while s < n:
        x = x + jnp.where(row >= s, pltpu.roll(x, s, axis=0), 0.0)
        s *= 2
    return x


def _block_row(x, blk, r):
    n, w = x.shape
    if blk >= SUBLANES:
        y = x.reshape(n // blk, blk, w)[:, r:r + 1, :]
        return jnp.broadcast_to(y, (n // blk, blk, w)).reshape(n, w)
    row = lax.broadcasted_iota(jnp.int32, x.shape, 0) & (SUBLANES - 1)
    if blk == 4:
        assert r == 1
        return jnp.where(row < 4, _block_row(x, SUBLANES, 1), _block_row(x, SUBLANES, 5))
    assert blk == 2 and r == 0
    return jnp.where((row & 1) == 1, pltpu.roll(x, 1, axis=0), x)


def _hgrn_chunk_stages(load, st_ref, seq, store):
    heads = [slice(hd * C_HEAD_K, (hd + 1) * C_HEAD_K) for hd in range(C_HEADS)]
    val = {}

    def masks(n):
        rr = lax.broadcasted_iota(jnp.int32, (n, n), 0)
        cc = lax.broadcasted_iota(jnp.int32, (n, n), 1)
        return rr, cc, rr ^ cc, rr > cc

    def prefix():
        q, k, logf = load('q'), load('k'), load('lf')
        n = q.shape[0]
        rr, cc, xr, lower = masks(n)
        p0 = q * k
        p1 = q * jnp.exp(logf) * pltpu.roll(k, 1, axis=0)
        scores = []
        for sl in heads:
            d0 = jnp.sum(p0[:, sl], axis=-1, keepdims=True)
            d1 = jnp.sum(p1[:, sl], axis=-1, keepdims=True)
            scores.append(jnp.where(rr == cc, d0, jnp.where((xr == 1) & lower, d1, 0.0)))
        val.update(q=q, k=k, b=_cumsum_rows(logf), scores=scores)

    def level(h):
        q, k, b = val['q'], val['k'], val['b']
        n = q.shape[0]
        _, _, xr, lower = masks(n)
        row = lax.broadcasted_iota(jnp.int32, (n, C_WIDTH), 0)
        bmid = _block_row(b, 2 * h, h - 1)
        m = jnp.where((row & h) != 0, q, k) * jnp.exp2(jnp.abs(b - bmid) * (-LOG2E))
        m = m.astype(BF16)
        pair = ((xr & -h) == h) & lower
        for hd, sl in enumerate(heads):
            val['scores'][hd] = jnp.where(pair, _dot_nt(m[:, sl], m[:, sl]), val['scores'][hd])

    def state():
        q, k, b = val['q'], val['k'], val['b']
        n = q.shape[0]
        vb = load('v').astype(BF16)
        b_last = b[n - 1:n, :]
        qb = (q * jnp.exp(b)).astype(BF16)
        kb = (k * jnp.exp(b_last - b)).astype(BF16)
        decay = jnp.exp(b_last)
        outs = []
        for hd, sl in enumerate(heads):
            st = st_ref[seq, hd]
            outs.append(_dot(val['scores'][hd].astype(BF16), vb[:, sl])
                        + _dot_nt(qb[:, sl], st.astype(BF16)))
            st_ref[seq, hd] = decay[:, sl] * st + _dot_tn(vb[:, sl], kb[:, sl])
        store(jnp.concatenate(outs, axis=1))

    return prefix, level, state


def _emit(*streams):
    order = []
    for k, stream in enumerate(streams):
        order += [((i + 0.5) / len(stream), k, i) for i in range(len(stream))]
    for _, k, i in sorted(order):
        streams[k][i]()


P_U, P_V, P_Q, P_F, P_I, P_G = (slice(i * C_WIDTH, (i + 1) * C_WIDTH) for i in range(6))


def _mixer_kernel(x_ref, pool0_ref, s0_ref, g_ref, win_ref, vgain_ref, ws_ref, bs_ref, wpool_ref,
                  pscale_ref, lbraw_ref, cgain_ref, wbr_ref, wout_ref,
                  xo_ref, poolo_ref, so_ref, *rest,
                  layer, n_seq, seq_tile, chunk, n_valid, emit_v, n_sub):
    if emit_v:
        vo_ref, rest = rest[0], rest[1:]
    p_s, k_s, ob_s, oa_s, oc_s, g_s, st_s, ext_s = rest
    rows = n_seq * seq_tile
    sub_rows = rows // n_sub
    t_idx = pl.program_id(1)

    @pl.when(t_idx == 0)
    def _init():
        for s in range(n_seq):
            for hd in range(C_HEADS):
                st_s[s, hd] = s0_ref[s, hd].T
            ext_s[s, 0:1, :] = jnp.zeros((1, B_WIDTH), F32)
            ext_s[s, POOL_PAD - POOL_BUF:POOL_PAD, :] = pool0_ref[s]

    lbr = lbraw_ref[...]
    e = jnp.exp(lbr - jnp.max(lbr, axis=0, keepdims=True))
    soft = e / jnp.sum(e, axis=0, keepdims=True)
    lb = jnp.zeros((1, C_WIDTH), F32)
    for j in range(1, layer + 1):
        lb = lb + soft[j:j + 1, :]

    tri = (lax.broadcasted_iota(jnp.int32, (chunk, chunk), 0)
           >= lax.broadcasted_iota(jnp.int32, (chunk, chunk), 1))
    w_mix = [jnp.where(tri, ws_ref[hd, 0:chunk, 0:chunk], 0.0).astype(BF16) for hd in range(A_HEADS)]

    def segments(j):
        if n_seq == 1:
            return [(0, j * sub_rows, j * sub_rows, sub_rows)]
        per = sub_rows // seq_tile
        return [(j * per + i, 0, (j * per + i) * seq_tile, seq_tile) for i in range(per)]

    def load_x(j):
        if n_seq == 1:
            return x_ref[0, j * sub_rows:(j + 1) * sub_rows, :]
        per = sub_rows // seq_tile
        return x_ref[j * per:(j + 1) * per].reshape(sub_rows, D_MODEL)

    def store_rows(ref, j, val):
        if n_seq == 1:
            ref[0, j * sub_rows:(j + 1) * sub_rows, :] = val
        else:
            per = sub_rows // seq_tile
            ref[j * per:(j + 1) * per] = val.reshape(per, seq_tile, val.shape[-1])

    class Sub:
        def __init__(self, j):
            self.j = j
            self.rs = slice(j * sub_rows, (j + 1) * sub_rows)
            self.h = None
            self.y = {}

        def norm(self):
            self.h = _rms(load_x(self.j), g_ref[2:3, :]).astype(BF16)

        def proj(self, off, width):
            return _dot(self.h, win_ref[:, off:off + width])

        def proj_thunks(self):
            rs = self.rs

            def to_p(slot, off):
                def run():
                    p_s[rs, slot] = self.proj(off, C_WIDTH)
                return run

            def pool_in():
                xb = self.proj(OFF_POOL, B_WIDTH)
                for s, off, r0, n in segments(self.j):
                    ext_s[s, POOL_PAD + off:POOL_PAD + off + n, :] = xb[r0 - rs.start:r0 - rs.start + n]

            return [to_p(P_U, OFF_UV), to_p(P_V, OFF_UV + A_WIDTH), pool_in,
                    to_p(P_Q, OFF_HGRN), to_p(P_F, OFF_HGRN + C_WIDTH),
                    to_p(P_I, OFF_HGRN + 2 * C_WIDTH), to_p(P_G, OFF_HGRN + 3 * C_WIDTH)]

        def prep_thunks(self):
            rs = self.rs

            def gelu_u():
                p_s[rs, P_U] = jax.nn.gelu(p_s[rs, P_U])

            def gelu_v():
                v = _rms(jax.nn.gelu(p_s[rs, P_V]), vgain_ref[...])
                p_s[rs, P_V] = v
                if emit_v:
                    store_rows(vo_ref, self.j, v)

            def pool():
                for s, off, r0, n in segments(self.j):
                    t_loc = lax.broadcasted_iota(jnp.int32, (n, 1), 0)
                    avail = t_idx * seq_tile + off + t_loc + (1 + n_valid)
                    parts = []
                    for g, w in enumerate(POOL_WINDOWS):
                        sl = slice(g * B_GROUP, (g + 1) * B_GROUP)
                        win = ext_s[s, off:off + POOL_PAD + n, sl]
                        x0 = win[POOL_PAD:]
                        d = 1
                        while d < w:
                            win = win + pltpu.roll(win, d, axis=0)
                            d *= 2
                        inv = 1.0 / jnp.minimum(avail, w).astype(F32)
                        pooled = win[POOL_PAD:] * inv - x0
                        parts.append(_dot(pooled.astype(BF16), wpool_ref[g].astype(BF16)))
                    ob_s[r0:r0 + n, :] = jnp.concatenate(parts, axis=1) * pscale_ref[...]

            def prep_q():
                p_s[rs, P_Q] = jax.nn.silu(p_s[rs, P_Q]) * (C_HEAD_K ** -0.5)

            def prep_f():
                z = p_s[rs, P_F]
                f = lb + (1.0 - lb) * jax.nn.sigmoid(z)
                p_s[rs, P_F] = jnp.log(jnp.maximum(f, F_FLOOR))
                k_s[rs, :] = (1.0 - lb) * jax.nn.sigmoid(-z)

            return [gelu_u, gelu_v, pool, prep_q, prep_f]

        def gate_thunks(self):
            rs = self.rs

            def piece(p):
                def run():
                    cols = slice(p * C_WIDTH, (p + 1) * C_WIDTH)
                    g_s[rs, cols] = self.proj(OFF_GATES + p * C_WIDTH, C_WIDTH)
                return run

            return [piece(p) for p in range(N_BRANCH * D_MODEL // C_WIDTH)]

        def chunk_thunks(self):
            out = []
            for c in range(self.rs.start // chunk, self.rs.stop // chunk):
                cs = slice(c * chunk, (c + 1) * chunk)

                def gmlp(cs=cs):
                    vb = p_s[cs, P_V].astype(BF16)
                    for hd in range(A_HEADS):
                        sl = slice(hd * A_HEAD, (hd + 1) * A_HEAD)
                        mixed = _dot(w_mix[hd], vb[:, sl]) + bs_ref[0:chunk, hd:hd + 1]
                        oa_s[cs, sl] = p_s[cs, sl] * mixed

                def load(name, cs=cs):
                    if name == 'k':
                        return k_s[cs, :]
                    return p_s[cs, {'q': P_Q, 'lf': P_F, 'v': P_I}[name]]

                def store(o, cs=cs):
                    oc_s[cs, :] = o

                prefix, level, state = _hgrn_chunk_stages(load, st_s, (c * chunk) // seq_tile, store)
                out += [gmlp, prefix]
                h = chunk // 2
                while h >= 2:
                    out.append(functools.partial(level, h))
                    h //= 2
                out.append(state)
            return out

        def branch_thunks(self):
            rs = self.rs

            def ya():
                self.y['a'] = _dot(oa_s[rs, :].astype(BF16), wbr_ref[0:A_WIDTH, :])

            def yb():
                self.y['b'] = _dot(ob_s[rs, :].astype(BF16), wbr_ref[A_WIDTH:A_WIDTH + B_WIDTH, :])

            def yc():
                oc = oc_s[rs, :]
                cg = cgain_ref[...]
                oc = jnp.concatenate(
                    [_rms(oc[:, hd * C_HEAD_V:(hd + 1) * C_HEAD_V], cg) for hd in range(C_HEADS)], axis=1)
                o_c = oc * jax.nn.silu(p_s[rs, P_G])
                self.y['c'] = _dot(o_c.astype(BF16), wbr_ref[A_WIDTH + B_WIDTH:, :])

            return [yb, ya, yc]

        def merge_thunks(self):
            rs = self.rs

            def merge():
                m = jax.nn.sigmoid(g_s[rs, 0:D_MODEL]) * self.y['a']
                m = m + jax.nn.sigmoid(g_s[rs, D_MODEL:2 * D_MODEL]) * self.y['b']
                m = m + jax.nn.sigmoid(g_s[rs, 2 * D_MODEL:]) * self.y['c']
                self.y['m'] = m.astype(BF16)

            def out():
                self.y['o'] = _dot(self.y['m'], wout_ref[...])

            return [merge, out]

        def finish(self):
            store_rows(xo_ref, self.j, load_x(self.j) + _rms(self.y['o'], g_ref[3:4, :]))

    subs = [Sub(j) for j in range(n_sub)]
    if n_sub == 1:
        a, = subs
        a.norm()
        _emit(a.proj_thunks(), a.prep_thunks())
        _emit(a.gate_thunks(), a.chunk_thunks())
        _emit(a.branch_thunks() + a.merge_thunks())
        a.finish()
    else:
        a, b = subs
        a.norm()
        _emit(a.proj_thunks())
        b.norm()
        _emit(b.proj_thunks(), a.prep_thunks())
        _emit(a.gate_thunks() + b.gate_thunks(), a.chunk_thunks() + b.prep_thunks())
        _emit(a.branch_thunks() + a.merge_thunks(), b.chunk_thunks())
        a.finish()
        _emit(b.branch_thunks() + b.merge_thunks())
        b.finish()

    for s in range(n_seq):
        poolo_ref[s] = ext_s[s, seq_tile + POOL_PAD - POOL_BUF:seq_tile + POOL_PAD, :]
        ext_s[s, 0:POOL_PAD, :] = ext_s[s, seq_tile:seq_tile + POOL_PAD, :]

    @pl.when(t_idx == pl.num_programs(1) - 1)
    def _final():
        for s in range(n_seq):
            for hd in range(C_HEADS):
                so_ref[s, hd] = st_s[s, hd].T


def _mixer(x, pool0, s0, gains, w_in, vgain, w_s, b_sT, w_pool, pscale, lb_raw, cgain, w_br, w_out,
           layer, n_seq, seq_tile, n_valid, emit_v):
    bsz, t, _ = x.shape
    chunk = min(seq_tile, GMLP_CHUNK)
    rows = n_seq * seq_tile
    grid = (bsz // n_seq, t // seq_tile)
    const = dict(pipeline_mode=pl.Buffered(1))

    def lsel(*tail):
        return lambda b, i: (layer,) + tail

    in_specs = [
        pl.BlockSpec((n_seq, seq_tile, D_MODEL), lambda b, i: (b, i, 0)),
        pl.BlockSpec((n_seq, POOL_BUF, B_WIDTH), lambda b, i: (b, 0, 0)),
        pl.BlockSpec((n_seq, C_HEADS, C_HEAD_K, C_HEAD_V), lambda b, i: (b, 0, 0, 0)),
        pl.BlockSpec((None, 6, D_MODEL), lsel(0, 0), **const),
        pl.BlockSpec((None, D_MODEL, IN_COLS), lsel(0, 0), **const),
        pl.BlockSpec((None, 1, A_WIDTH), lsel(0, 0), **const),
        pl.BlockSpec((None, A_HEADS, GMLP_CHUNK, GMLP_CHUNK), lsel(0, 0, 0), **const),
        pl.BlockSpec((None, GMLP_CHUNK, A_HEADS), lsel(0, 0), **const),
        pl.BlockSpec((None, B_GROUPS, B_GROUP, B_GROUP), lsel(0, 0, 0), **const),
        pl.BlockSpec((None, 1, B_WIDTH), lsel(0, 0), **const),
        pl.BlockSpec((DEPTH, C_WIDTH), lambda b, i: (0, 0), **const),
        pl.BlockSpec((None, 1, C_HEAD_V), lsel(0, 0), **const),
        pl.BlockSpec((None, N_BRANCH * A_WIDTH, D_MODEL), lsel(0, 0), **const),
        pl.BlockSpec((None, D_MODEL, D_MODEL), lsel(0, 0), **const),
    ]
    out_specs = [
        pl.BlockSpec((n_seq, seq_tile, D_MODEL), lambda b, i: (b, i, 0)),
        pl.BlockSpec((n_seq, POOL_BUF, B_WIDTH), lambda b, i: (b, 0, 0)),
        pl.BlockSpec((n_seq, C_HEADS, C_HEAD_K, C_HEAD_V), lambda b, i: (b, 0, 0, 0)),
    ]
    out_shape = [
        jax.ShapeDtypeStruct(x.shape, F32),
        jax.ShapeDtypeStruct((bsz, POOL_BUF, B_WIDTH), F32),
        jax.ShapeDtypeStruct((bsz, C_HEADS, C_HEAD_K, C_HEAD_V), F32),
    ]
    if emit_v:
        out_specs.append(pl.BlockSpec((n_seq, seq_tile, A_WIDTH), lambda b, i: (b, i, 0)))
        out_shape.append(jax.ShapeDtypeStruct((bsz, t, A_WIDTH), F32))
    act = pltpu.VMEM((rows, C_WIDTH), F32)
    scratch = [pltpu.VMEM((rows, 6 * C_WIDTH), F32)] + [act] * 4 + [
        pltpu.VMEM((rows, N_BRANCH * D_MODEL), F32),
        pltpu.VMEM((n_seq, C_HEADS, C_HEAD_V, C_HEAD_K), F32),
        pltpu.VMEM((n_seq, POOL_PAD + seq_tile, B_WIDTH), F32),
    ]
    return pl.pallas_call(
        functools.partial(_mixer_kernel, layer=layer, n_seq=n_seq, seq_tile=seq_tile, chunk=chunk,
                          n_valid=n_valid, emit_v=emit_v, n_sub=min(2, max(1, rows // MIX_SUB_ROWS))),
        grid=grid,
        in_specs=in_specs,
        out_specs=out_specs,
        out_shape=out_shape,
        scratch_shapes=scratch,
        compiler_params=pltpu.CompilerParams(
            dimension_semantics=("arbitrary", "arbitrary"), vmem_limit_bytes=VMEM_LIMIT),
        name=f"mixer_l{layer}_{'dec' if emit_v else 'pre'}",
    )(x, pool0, s0, gains, w_in, vgain, w_s, b_sT, w_pool, pscale, lb_raw, cgain, w_br, w_out)


def kernel(x_prompt, x_sample, state_pool, state_hgrn, norm_gains, w_ffn1_gu, w_ffn1_down, w_in,
           gmlp_v_gain, gmlp_w_s, gmlp_b_s, pool_w, pool_scale, hgrn_lower_bounds, hgrn_out_gain,
           w_branch, w_out, w_ffn2_gu, w_ffn2_down):
    bp, tp, _ = x_prompt.shape
    bs, ts, _ = x_sample.shape
    w1gu, w1dn = w_ffn1_gu.astype(BF16), w_ffn1_down.astype(BF16)
    w2gu, w2dn = w_ffn2_gu.astype(BF16), w_ffn2_down.astype(BF16)
    win, wbr, wout = w_in.astype(BF16), w_branch.astype(BF16), w_out.astype(BF16)
    vgain = gmlp_v_gain.reshape(DEPTH, 1, A_WIDTH)
    pscale = pool_scale.reshape(DEPTH, 1, B_WIDTH)
    cgain = hgrn_out_gain.reshape(DEPTH, 1, C_HEAD_V)
    b_sT = jnp.swapaxes(gmlp_b_s, 1, 2)
    zero_pool = jnp.zeros((bp, POOL_BUF, B_WIDTH), F32)
    zero_state = jnp.zeros((bp, C_HEADS, C_HEAD_K, C_HEAD_V), F32)

    xp, xs = x_prompt, x_sample
    pool_p, pool_s, hg_p, hg_s, v_s = [], [], [], [], []
    for l in range(DEPTH):
        shared = (norm_gains, win, vgain, gmlp_w_s, b_sT, pool_w, pscale, hgrn_lower_bounds, cgain,
                  wbr, wout)
        xp = _ffn(xp.reshape(bp * tp, D_MODEL), norm_gains, w1gu, w1dn, l, 0, 1).reshape(bp, tp, D_MODEL)
        xs = _ffn(xs.reshape(bs * ts, D_MODEL), norm_gains, w1gu, w1dn, l, 0, 1).reshape(bs, ts, D_MODEL)
        xp, pp, sp = _mixer(xp, zero_pool, zero_state, *shared, layer=l, n_seq=1,
                            seq_tile=min(MIX_ROWS, tp), n_valid=0, emit_v=False)
        xs, ps, ss, vs = _mixer(xs, state_pool[l], state_hgrn[l], *shared, layer=l, n_seq=bs,
                                seq_tile=ts, n_valid=POOL_BUF, emit_v=True)
        xp = _ffn(xp.reshape(bp * tp, D_MODEL), norm_gains, w2gu, w2dn, l, 4, 5).reshape(bp, tp, D_MODEL)
        xs = _ffn(xs.reshape(bs * ts, D_MODEL), norm_gains, w2gu, w2dn, l, 4, 5).reshape(bs, ts, D_MODEL)
        pool_p.append(pp)
        pool_s.append(ps)
        hg_p.append(sp)
        hg_s.append(ss)
        v_s.append(vs)
    return (xp, xs, jnp.stack(pool_p), jnp.stack(pool_s), jnp.stack(hg_p), jnp.stack(hg_s),
            jnp.stack(v_s))
```

```python
import functools

import jax
import jax.numpy as jnp
from jax import lax
from jax.experimental import pallas as pl
from jax.experimental.pallas import tpu as pltpu

D_MODEL = 1024
DEPTH = 4
D_FF = 2816
A_HEADS = 4
A_HEAD = 128
A_WIDTH = A_HEADS * A_HEAD
GMLP_CHUNK = 128
B_GROUPS = 4
B_GROUP = 128
B_WIDTH = B_GROUPS * B_GROUP
POOL_WINDOWS = (2, 4, 8, 16)
POOL_BUF = 15
POOL_PAD = 16
C_HEADS = 4
C_HEAD_K = 128
C_HEAD_V = 128
C_WIDTH = C_HEADS * C_HEAD_V
N_BRANCH = 3
EPS = 1e-6
F_FLOOR = 1e-30
LOG2E = 1.4426950408889634

OFF_UV = 0
OFF_POOL = 2 * A_WIDTH
OFF_HGRN = OFF_POOL + B_WIDTH
OFF_GATES = OFF_HGRN + 4 * C_WIDTH
IN_COLS = OFF_GATES + N_BRANCH * D_MODEL

SUBLANES = 8
FFN_ROWS = 1024
FFN_SUB_ROWS = 256
MIX_ROWS = 512
MIX_SUB_ROWS = 512
VMEM_LIMIT = 56 * 1024 * 1024

F32 = jnp.float32
BF16 = jnp.bfloat16


def _rms(x, g):
    ms = jnp.mean(x * x, axis=-1, keepdims=True)
    return x * lax.rsqrt(ms + EPS) * g


def _dot(a, b):
    return jnp.dot(a, b, preferred_element_type=F32)


def _dot_nt(a, b):
    return lax.dot_general(a, b, (((1,), (1,)), ((), ())), preferred_element_type=F32)


def _dot_tn(a, b):
    return lax.dot_general(a, b, (((0,), (0,)), ((), ())), preferred_element_type=F32)


def _ffn_kernel(x_ref, xn_ref, g_ref, wgu_ref, wdn_ref, o_ref, h0_s, *, pre, post, n_sub):
    sub = x_ref.shape[0] // n_sub
    g_pre = g_ref[pre:pre + 1, :]

    @pl.when(pl.program_id(0) == 0)
    def _first():
        h0_s[...] = _rms(x_ref[0:sub, :], g_pre).astype(BF16)

    for s in range(n_sub):
        rs = slice(s * sub, (s + 1) * sub)
        x = x_ref[rs, :]
        h = h0_s[...] if s == 0 else _rms(x, g_pre).astype(BF16)
        g = _dot(h, wgu_ref[:, :D_FF])
        u = _dot(h, wgu_ref[:, D_FF:])
        a = (jax.nn.silu(g) * u).astype(BF16)
        y = _dot(a, wdn_ref[...])
        o_ref[rs, :] = x + 0.5 * _rms(y, g_ref[post:post + 1, :])
    h0_s[...] = _rms(xn_ref[...], g_pre).astype(BF16)


def _ffn(x2d, gains, wgu, wdn, layer, pre, post):
    rows = x2d.shape[0]
    tm = min(FFN_ROWS, rows)
    n_sub = max(1, tm // FFN_SUB_ROWS)
    sub = tm // n_sub
    n_steps = rows // tm
    const = dict(pipeline_mode=pl.Buffered(1))
    return pl.pallas_call(
        functools.partial(_ffn_kernel, pre=pre, post=post, n_sub=n_sub),
        grid=(n_steps,),
        in_specs=[
            pl.BlockSpec((tm, D_MODEL), lambda i: (i, 0)),
            pl.BlockSpec((sub, D_MODEL), lambda i: (jnp.minimum(i + 1, n_steps - 1) * n_sub, 0)),
            pl.BlockSpec((None, 6, D_MODEL), lambda i: (layer, 0, 0), **const),
            pl.BlockSpec((None, D_MODEL, 2 * D_FF), lambda i: (layer, 0, 0), **const),
            pl.BlockSpec((None, D_FF, D_MODEL), lambda i: (layer, 0, 0), **const),
        ],
        out_specs=pl.BlockSpec((tm, D_MODEL), lambda i: (i, 0)),
        out_shape=jax.ShapeDtypeStruct(x2d.shape, F32),
        scratch_shapes=[pltpu.VMEM((sub, D_MODEL), BF16)],
        compiler_params=pltpu.CompilerParams(
            dimension_semantics=("arbitrary",), vmem_limit_bytes=VMEM_LIMIT),
        name=f"ffn_l{layer}_{pre}",
    )(x2d, x2d, gains, wgu, wdn)


def _cumsum_rows(x):
    n = x.shape[0]
    row = lax.broadcasted_iota(jnp.int32, x.shape, 0)
    s = 1
    while s < n:
        x = x + jnp.where(row >= s, pltpu.roll(x, s, axis=0), 0.0)
        s *= 2
    return x


def _block_row(x, blk, r):
    n, w = x.shape
    if blk >= SUBLANES:
        y = x.reshape(n // blk, blk, w)[:, r:r + 1, :]
        return jnp.broadcast_to(y, (n // blk, blk, w)).reshape(n, w)
    row = lax.broadcasted_iota(jnp.int32, x.shape, 0) & (SUBLANES - 1)
    if blk == 4:
        assert r == 1
        return jnp.where(row < 4, _block_row(x, SUBLANES, 1), _block_row(x, SUBLANES, 5))
    assert blk == 2 and r == 0
    return jnp.where((row & 1) == 1, pltpu.roll(x, 1, axis=0), x)


def _hgrn_chunk_stages(load, st_ref, seq, store):
    heads = [slice(hd * C_HEAD_K, (hd + 1) * C_HEAD_K) for hd in range(C_HEADS)]
    val = {}

    def masks(n):
        rr = lax.broadcasted_iota(jnp.int32, (n, n), 0)
        cc = lax.broadcasted_iota(jnp.int32, (n, n), 1)
        return rr, cc, rr ^ cc, rr > cc

    def prefix():
        q, k, logf = load('q'), load('k'), load('lf')
        n = q.shape[0]
        rr, cc, xr, lower = masks(n)
        p0 = q * k
        p1 = q * jnp.exp(logf) * pltpu.roll(k, 1, axis=0)
        scores = []
        for sl in heads:
            d0 = jnp.sum(p0[:, sl], axis=-1, keepdims=True)
            d1 = jnp.sum(p1[:, sl], axis=-1, keepdims=True)
            scores.append(jnp.where(rr == cc, d0, jnp.where((xr == 1) & lower, d1, 0.0)))
        val.update(q=q, k=k, b=_cumsum_rows(logf), scores=scores)

    def level(h):
        q, k, b = val['q'], val['k'], val['b']
        n = q.shape[0]
        _, _, xr, lower = masks(n)
        row = lax.broadcasted_iota(jnp.int32, (n, C_WIDTH), 0)
        bmid = _block_row(b, 2 * h, h - 1)
        m = jnp.where((row & h) != 0, q, k) * jnp.exp2(jnp.abs(b - bmid) * (-LOG2E))
        m = m.astype(BF16)
        pair = ((xr & -h) == h) & lower
        for hd, sl in enumerate(heads):
            val['scores'][hd] = jnp.where(pair, _dot_nt(m[:, sl], m[:, sl]), val['scores'][hd])

    def state():
        q, k, b = val['q'], val['k'], val['b']
        n = q.shape[0]
        vb = load('v').astype(BF16)
        b_last = b[n - 1:n, :]
        qb = (q * jnp.exp(b)).astype(BF16)
        kb = (k * jnp.exp(b_last - b)).astype(BF16)
        decay = jnp.exp(b_last)
        outs = []
        for hd, sl in enumerate(heads):
            st = st_ref[seq, hd]
            outs.append(_dot(val['scores'][hd].astype(BF16), vb[:, sl])
                        + _dot_nt(qb[:, sl], st.astype(BF16)))
            st_ref[seq, hd] = decay[:, sl] * st + _dot_tn(vb[:, sl], kb[:, sl])
        store(jnp.concatenate(outs, axis=1))

    return prefix, level, state


def _emit(*streams):
    order = []
    for k, stream in enumerate(streams):
        order += [((i + 0.5) / len(stream), k, i) for i in range(len(stream))]
    for _, k, i in sorted(order):
        streams[k][i]()


P_U, P_V, P_Q, P_F, P_I, P_G = (slice(i * C_WIDTH, (i + 1) * C_WIDTH) for i in range(6))


def _mixer_kernel(x_ref, pool0_ref, s0_ref, g_ref, win_ref, vgain_ref, ws_ref, bs_ref, wpool_ref,
                  pscale_ref, lbraw_ref, cgain_ref, wbr_ref, wout_ref,
                  xo_ref, poolo_ref, so_ref, *rest,
                  layer, n_seq, seq_tile, chunk, n_valid, emit_v, n_sub):
    if emit_v:
        vo_ref, rest = rest[0], rest[1:]
    p_s, k_s, ob_s, oa_s, oc_s, g_s, st_s, ext_s = rest
    rows = n_seq * seq_tile
    sub_rows = rows // n_sub
    t_idx = pl.program_id(1)

    @pl.when(t_idx == 0)
    def _init():
        for s in range(n_seq):
            for hd in range(C_HEADS):
                st_s[s, hd] = s0_ref[s, hd].T
            ext_s[s, 0:1, :] = jnp.zeros((1, B_WIDTH), F32)
            ext_s[s, POOL_PAD - POOL_BUF:POOL_PAD, :] = pool0_ref[s]

    lbr = lbraw_ref[...]
    e = jnp.exp(lbr - jnp.max(lbr, axis=0, keepdims=True))
    soft = e / jnp.sum(e, axis=0, keepdims=True)
    lb = jnp.zeros((1, C_WIDTH), F32)
    for j in range(1, layer + 1):
        lb = lb + soft[j:j + 1, :]

    tri = (lax.broadcasted_iota(jnp.int32, (chunk, chunk), 0)
           >= lax.broadcasted_iota(jnp.int32, (chunk, chunk), 1))
    w_mix = [jnp.where(tri, ws_ref[hd, 0:chunk, 0:chunk], 0.0).astype(BF16) for hd in range(A_HEADS)]

    def segments(j):
        if n_seq == 1:
            return [(0, j * sub_rows, j * sub_rows, sub_rows)]
        per = sub_rows // seq_tile
        return [(j * per + i, 0, (j * per + i) * seq_tile, seq_tile) for i in range(per)]

    def load_x(j):
        if n_seq == 1:
            return x_ref[0, j * sub_rows:(j + 1) * sub_rows, :]
        per = sub_rows // seq_tile
        return x_ref[j * per:(j + 1) * per].reshape(sub_rows, D_MODEL)

    def store_rows(ref, j, val):
        if n_seq == 1:
            ref[0, j * sub_rows:(j + 1) * sub_rows, :] = val
        else:
            per = sub_rows // seq_tile
            ref[j * per:(j + 1) * per] = val.reshape(per, seq_tile, val.shape[-1])

    class Sub:
        def __init__(self, j):
            self.j = j
            self.rs = slice(j * sub_rows, (j + 1) * sub_rows)
            self.h = None
            self.y = {}

        def norm(self):
            self.h = _rms(load_x(self.j), g_ref[2:3, :]).astype(BF16)

        def proj(self, off, width):
            return _dot(self.h, win_ref[:, off:off + width])

        def proj_thunks(self):
            rs = self.rs

            def to_p(slot, off):
                def run():
                    p_s[rs, slot] = self.proj(off, C_WIDTH)
                return run

            def pool_in():
                xb = self.proj(OFF_POOL, B_WIDTH)
                for s, off, r0, n in segments(self.j):
                    ext_s[s, POOL_PAD + off:POOL_PAD + off + n, :] = xb[r0 - rs.start:r0 - rs.start + n]

            return [to_p(P_U, OFF_UV), to_p(P_V, OFF_UV + A_WIDTH), pool_in,
                    to_p(P_Q, OFF_HGRN), to_p(P_F, OFF_HGRN + C_WIDTH),
                    to_p(P_I, OFF_HGRN + 2 * C_WIDTH), to_p(P_G, OFF_HGRN + 3 * C_WIDTH)]

        def prep_thunks(self):
            rs = self.rs

            def gelu_u():
                p_s[rs, P_U] = jax.nn.gelu(p_s[rs, P_U])

            def gelu_v():
                v = _rms(jax.nn.gelu(p_s[rs, P_V]), vgain_ref[...])
                p_s[rs, P_V] = v
                if emit_v:
                    store_rows(vo_ref, self.j, v)

            def pool():
                for s, off, r0, n in segments(self.j):
                    t_loc = lax.broadcasted_iota(jnp.int32, (n, 1), 0)
                    avail = t_idx * seq_tile + off + t_loc + (1 + n_valid)
                    parts = []
                    for g, w in enumerate(POOL_WINDOWS):
                        sl = slice(g * B_GROUP, (g + 1) * B_GROUP)
                        win = ext_s[s, off:off + POOL_PAD + n, sl]
                        x0 = win[POOL_PAD:]
                        d = 1
                        while d < w:
                            win = win + pltpu.roll(win, d, axis=0)
                            d *= 2
                        inv = 1.0 / jnp.minimum(avail, w).astype(F32)
                        pooled = win[POOL_PAD:] * inv - x0
                        parts.append(_dot(pooled.astype(BF16), wpool_ref[g].astype(BF16)))
                    ob_s[r0:r0 + n, :] = jnp.concatenate(parts, axis=1) * pscale_ref[...]

            def prep_q():
                p_s[rs, P_Q] = jax.nn.silu(p_s[rs, P_Q]) * (C_HEAD_K ** -0.5)

            def prep_f():
                z = p_s[rs, P_F]
                f = lb + (1.0 - lb) * jax.nn.sigmoid(z)
                p_s[rs, P_F] = jnp.log(jnp.maximum(f, F_FLOOR))
                k_s[rs, :] = (1.0 - lb) * jax.nn.sigmoid(-z)

            return [gelu_u, gelu_v, pool, prep_q, prep_f]

        def gate_thunks(self):
            rs = self.rs

            def piece(p):
                def run():
                    cols = slice(p * C_WIDTH, (p + 1) * C_WIDTH)
                    g_s[rs, cols] = self.proj(OFF_GATES + p * C_WIDTH, C_WIDTH)
                return run

            return [piece(p) for p in range(N_BRANCH * D_MODEL // C_WIDTH)]

        def chunk_thunks(self):
            out = []
            for c in range(self.rs.start // chunk, self.rs.stop // chunk):
                cs = slice(c * chunk, (c + 1) * chunk)

                def gmlp(cs=cs):
                    vb = p_s[cs, P_V].astype(BF16)
                    for hd in range(A_HEADS):
                        sl = slice(hd * A_HEAD, (hd + 1) * A_HEAD)
                        mixed = _dot(w_mix[hd], vb[:, sl]) + bs_ref[0:chunk, hd:hd + 1]
                        oa_s[cs, sl] = p_s[cs, sl] * mixed

                def load(name, cs=cs):
                    if name == 'k':
                        return k_s[cs, :]
                    return p_s[cs, {'q': P_Q, 'lf': P_F, 'v': P_I}[name]]

                def store(o, cs=cs):
                    oc_s[cs, :] = o

                prefix, level, state = _hgrn_chunk_stages(load, st_s, (c * chunk) // seq_tile, store)
                out += [gmlp, prefix]
                h = chunk // 2
                while h >= 2:
                    out.append(functools.partial(level, h))
                    h //= 2
                out.append(state)
            return out

        def branch_thunks(self):
            rs = self.rs

            def ya():
                self.y['a'] = _dot(oa_s[rs, :].astype(BF16), wbr_ref[0:A_WIDTH, :])

            def yb():
                self.y['b'] = _dot(ob_s[rs, :].astype(BF16), wbr_ref[A_WIDTH:A_WIDTH + B_WIDTH, :])

            def yc():
                oc = oc_s[rs, :]
                cg = cgain_ref[...]
                oc = jnp.concatenate(
                    [_rms(oc[:, hd * C_HEAD_V:(hd + 1) * C_HEAD_V], cg) for hd in range(C_HEADS)], axis=1)
                o_c = oc * jax.nn.silu(p_s[rs, P_G])
                self.y['c'] = _dot(o_c.astype(BF16), wbr_ref[A_WIDTH + B_WIDTH:, :])

            return [yb, ya, yc]

        def merge_thunks(self):
            rs = self.rs

            def merge():
                m = jax.nn.sigmoid(g_s[rs, 0:D_MODEL]) * self.y['a']
                m = m + jax.nn.sigmoid(g_s[rs, D_MODEL:2 * D_MODEL]) * self.y['b']
                m = m + jax.nn.sigmoid(g_s[rs, 2 * D_MODEL:]) * self.y['c']
                self.y['m'] = m.astype(BF16)

            def out():
                self.y['o'] = _dot(self.y['m'], wout_ref[...])

            return [merge, out]

        def finish(self):
            store_rows(xo_ref, self.j, load_x(self.j) + _rms(self.y['o'], g_ref[3:4, :]))

    subs = [Sub(j) for j in range(n_sub)]
    if n_sub == 1:
        a, = subs
        a.norm()
        _emit(a.proj_thunks(), a.prep_thunks())
        _emit(a.gate_thunks(), a.chunk_thunks())
        _emit(a.branch_thunks() + a.merge_thunks())
        a.finish()
    else:
        a, b = subs
        a.norm()
        _emit(a.proj_thunks())
        b.norm()
        _emit(b.proj_thunks(), a.prep_thunks())
        _emit(a.gate_thunks() + b.gate_thunks(), a.chunk_thunks() + b.prep_thunks())
        _emit(a.branch_thunks() + a.merge_thunks(), b.chunk_thunks())
        a.finish()
        _emit(b.branch_thunks() + b.merge_thunks())
        b.finish()

    for s in range(n_seq):
        poolo_ref[s] = ext_s[s, seq_tile + POOL_PAD - POOL_BUF:seq_tile + POOL_PAD, :]
        ext_s[s, 0:POOL_PAD, :] = ext_s[s, seq_tile:seq_tile + POOL_PAD, :]

    @pl.when(t_idx == pl.num_programs(1) - 1)
    def _final():
        for s in range(n_seq):
            for hd in range(C_HEADS):
                so_ref[s, hd] = st_s[s, hd].T


def _mixer(x, pool0, s0, gains, w_in, vgain, w_s, b_sT, w_pool, pscale, lb_raw, cgain, w_br, w_out,
           layer, n_seq, seq_tile, n_valid, emit_v):
    bsz, t, _ = x.shape
    chunk = min(seq_tile, GMLP_CHUNK)
    rows = n_seq * seq_tile
    grid = (bsz // n_seq, t // seq_tile)
    const = dict(pipeline_mode=pl.Buffered(1))

    def lsel(*tail):
        return lambda b, i: (layer,) + tail

    in_specs = [
        pl.BlockSpec((n_seq, seq_tile, D_MODEL), lambda b, i: (b, i, 0)),
        pl.BlockSpec((n_seq, POOL_BUF, B_WIDTH), lambda b, i: (b, 0, 0)),
        pl.BlockSpec((n_seq, C_HEADS, C_HEAD_K, C_HEAD_V), lambda b, i: (b, 0, 0, 0)),
        pl.BlockSpec((None, 6, D_MODEL), lsel(0, 0), **const),
        pl.BlockSpec((None, D_MODEL, IN_COLS), lsel(0, 0), **const),
        pl.BlockSpec((None, 1, A_WIDTH), lsel(0, 0), **const),
        pl.BlockSpec((None, A_HEADS, GMLP_CHUNK, GMLP_CHUNK), lsel(0, 0, 0), **const),
        pl.BlockSpec((None, GMLP_CHUNK, A_HEADS), lsel(0, 0), **const),
        pl.BlockSpec((None, B_GROUPS, B_GROUP, B_GROUP), lsel(0, 0, 0), **const),
        pl.BlockSpec((None, 1, B_WIDTH), lsel(0, 0), **const),
        pl.BlockSpec((DEPTH, C_WIDTH), lambda b, i: (0, 0), **const),
        pl.BlockSpec((None, 1, C_HEAD_V), lsel(0, 0), **const),
        pl.BlockSpec((None, N_BRANCH * A_WIDTH, D_MODEL), lsel(0, 0), **const),
        pl.BlockSpec((None, D_MODEL, D_MODEL), lsel(0, 0), **const),
    ]
    out_specs = [
        pl.BlockSpec((n_seq, seq_tile, D_MODEL), lambda b, i: (b, i, 0)),
        pl.BlockSpec((n_seq, POOL_BUF, B_WIDTH), lambda b, i: (b, 0, 0)),
        pl.BlockSpec((n_seq, C_HEADS, C_HEAD_K, C_HEAD_V), lambda b, i: (b, 0, 0, 0)),
    ]
    out_shape = [
        jax.ShapeDtypeStruct(x.shape, F32),
        jax.ShapeDtypeStruct((bsz, POOL_BUF, B_WIDTH), F32),
        jax.ShapeDtypeStruct((bsz, C_HEADS, C_HEAD_K, C_HEAD_V), F32),
    ]
    if emit_v:
        out_specs.append(pl.BlockSpec((n_seq, seq_tile, A_WIDTH), lambda b, i: (b, i, 0)))
        out_shape.append(jax.ShapeDtypeStruct((bsz, t, A_WIDTH), F32))
    act = pltpu.VMEM((rows, C_WIDTH), F32)
    scratch = [pltpu.VMEM((rows, 6 * C_WIDTH), F32)] + [act] * 4 + [
        pltpu.VMEM((rows, N_BRANCH * D_MODEL), F32),
        pltpu.VMEM((n_seq, C_HEADS, C_HEAD_V, C_HEAD_K), F32),
        pltpu.VMEM((n_seq, POOL_PAD + seq_tile, B_WIDTH), F32),
    ]
    return pl.pallas_call(
        functools.partial(_mixer_kernel, layer=layer, n_seq=n_seq, seq_tile=seq_tile, chunk=chunk,
                          n_valid=n_valid, emit_v=emit_v, n_sub=min(2, max(1, rows // MIX_SUB_ROWS))),
        grid=grid,
        in_specs=in_specs,
        out_specs=out_specs,
        out_shape=out_shape,
        scratch_shapes=scratch,
        compiler_params=pltpu.CompilerParams(
            dimension_semantics=("arbitrary", "arbitrary"), vmem_limit_bytes=VMEM_LIMIT),
        name=f"mixer_l{layer}_{'dec' if emit_v else 'pre'}",
    )(x, pool0, s0, gains, w_in, vgain, w_s, b_sT, w_pool, pscale, lb_raw, cgain, w_br, w_out)


def kernel(x_prompt, x_sample, state_pool, state_hgrn, norm_gains, w_ffn1_gu, w_ffn1_down, w_in,
           gmlp_v_gain, gmlp_w_s, gmlp_b_s, pool_w, pool_scale, hgrn_lower_bounds, hgrn_out_gain,
           w_branch, w_out, w_ffn2_gu, w_ffn2_down):
    bp, tp, _ = x_prompt.shape
    bs, ts, _ = x_sample.shape
    w1gu, w1dn = w_ffn1_gu.astype(BF16), w_ffn1_down.astype(BF16)
    w2gu, w2dn = w_ffn2_gu.astype(BF16), w_ffn2_down.astype(BF16)
    win, wbr, wout = w_in.astype(BF16), w_branch.astype(BF16), w_out.astype(BF16)
    vgain = gmlp_v_gain.reshape(DEPTH, 1, A_WIDTH)
    pscale = pool_scale.reshape(DEPTH, 1, B_WIDTH)
    cgain = hgrn_out_gain.reshape(DEPTH, 1, C_HEAD_V)
    b_sT = jnp.swapaxes(gmlp_b_s, 1, 2)
    zero_pool = jnp.zeros((bp, POOL_BUF, B_WIDTH), F32)
    zero_state = jnp.zeros((bp, C_HEADS, C_HEAD_K, C_HEAD_V), F32)

    xp, xs = x_prompt, x_sample
    pool_p, pool_s, hg_p, hg_s, v_s = [], [], [], [], []
    for l in range(DEPTH):
        shared = (norm_gains, win, vgain, gmlp_w_s, b_sT, pool_w, pscale, hgrn_lower_bounds, cgain,
                  wbr, wout)
        xp = _ffn(xp.reshape(bp * tp, D_MODEL), norm_gains, w1gu, w1dn, l, 0, 1).reshape(bp, tp, D_MODEL)
        xs = _ffn(xs.reshape(bs * ts, D_MODEL), norm_gains, w1gu, w1dn, l, 0, 1).reshape(bs, ts, D_MODEL)
        xp, pp, sp = _mixer(xp, zero_pool, zero_state, *shared, layer=l, n_seq=1,
                            seq_tile=min(MIX_ROWS, tp), n_valid=0, emit_v=False)
        xs, ps, ss, vs = _mixer(xs, state_pool[l], state_hgrn[l], *shared, layer=l, n_seq=bs,
                                seq_tile=ts, n_valid=POOL_BUF, emit_v=True)
        xp = _ffn(xp.reshape(bp * tp, D_MODEL), norm_gains, w2gu, w2dn, l, 4, 5).reshape(bp, tp, D_MODEL)
        xs = _ffn(xs.reshape(bs * ts, D_MODEL), norm_gains, w2gu, w2dn, l, 4, 5).reshape(bs, ts, D_MODEL)
        pool_p.append(pp)
        pool_s.append(ps)
        hg_p.append(sp)
        hg_s.append(ss)
        v_s.append(vs)
    return (xp, xs, jnp.stack(pool_p), jnp.stack(pool_s), jnp.stack(hg_p), jnp.stack(hg_s),
            jnp.stack(v_s))
```
